```python
import jax, jax.numpy as jnp
from jax import lax
import numpy as np

D_MODEL = 1024
BATCH = 8
SEQ = 2048
DEPTH = 4

CONV_WIDTH = D_MODEL // 2
CONV_HEADS = 8
CONV_K = 3
POOL_WIDTH = D_MODEL // 2
POOL_WINDOWS = (2, 4, 8, 16)
POOL_GROUPS = len(POOL_WINDOWS)
POOL_GROUP_DIM = POOL_WIDTH // POOL_GROUPS
MIX_WIDTH = CONV_WIDTH + POOL_WIDTH
IN_COLS = 4 * CONV_WIDTH + 2 * POOL_WIDTH
NORM_EPS = 1e-6

kernel_name = "hybrid_shortconv_pool_parallel_adaln"


def rms_norm(x, g):
    xf = x.astype(jnp.float32)
    y = xf * lax.rsqrt(jnp.mean(xf * xf, axis=-1, keepdims=True) + NORM_EPS)
    return (y * g.astype(jnp.float32)).astype(x.dtype)


def causal_depthwise_conv(z, w):
    C = z.shape[-1]
    return lax.conv_general_dilated(
        z, w[:, None, :].astype(z.dtype),
        window_strides=(1,), padding=[(CONV_K - 1, 0)],
        dimension_numbers=("NWC", "WIO", "NWC"),
        feature_group_count=C)


def causal_multiscale_pool(p, w_pool, pool_scale):
    B, T, _ = p.shape
    pf = p.astype(jnp.float32)
    pos = jnp.arange(1, T + 1, dtype=jnp.float32)[None, :, None]
    outs = []
    for g, w in enumerate(POOL_WINDOWS):
        pg = pf[..., g * POOL_GROUP_DIM:(g + 1) * POOL_GROUP_DIM]
        s = jnp.cumsum(pg, axis=1)
        lag = jnp.pad(s, ((0, 0), (w, 0), (0, 0)))[:, :T]
        mean = (s - lag) / jnp.minimum(pos, float(w))
        outs.append(mean - pg)
    pooled = jnp.stack(outs, axis=2).astype(p.dtype)
    mixed = jnp.einsum("btgc,gcd->btgd", pooled, w_pool)
    return mixed.reshape(B, T, POOL_WIDTH) * pool_scale


def setup_inputs(seed: int = 0) -> dict:
    key = jax.random.key(seed)
    ks = jax.random.split(key, 12)
    f32 = jnp.float32
    x = jax.random.normal(ks[0], (BATCH, SEQ, D_MODEL), f32)
    c = jax.random.normal(ks[1], (BATCH, D_MODEL), f32)
    w_ada = jax.random.normal(ks[2], (DEPTH, D_MODEL, 3 * D_MODEL), f32) * D_MODEL ** -0.5
    b_ada = 0.01 * jax.random.normal(ks[3], (DEPTH, 3 * D_MODEL), f32)
    g_pre = 1.0 + 0.02 * jax.random.normal(ks[4], (DEPTH, D_MODEL), f32)
    w_in = jax.random.normal(ks[5], (DEPTH, D_MODEL, IN_COLS), f32) * D_MODEL ** -0.5
    w_conv = jax.random.normal(ks[6], (DEPTH, CONV_K, CONV_WIDTH), f32) * CONV_K ** -0.5
    w_pool = jax.random.normal(ks[7], (DEPTH, POOL_GROUPS, POOL_GROUP_DIM, POOL_GROUP_DIM), f32) * POOL_GROUP_DIM ** -0.5
    pool_scale = 1.0 + 0.1 * jax.random.normal(ks[8], (DEPTH, POOL_WIDTH), f32)
    w_out = jax.random.normal(ks[9], (DEPTH, MIX_WIDTH, D_MODEL), f32) * MIX_WIDTH ** -0.5
    g_post = 1.0 + 0.02 * jax.random.normal(ks[10], (DEPTH, D_MODEL), f32)
    return {"x": x, "c": c, "w_ada": w_ada, "b_ada": b_ada, "g_pre": g_pre, "w_in": w_in,
            "w_conv": w_conv, "w_pool": w_pool, "pool_scale": pool_scale,
            "w_out": w_out, "g_post": g_post}


def reference(x, c, w_ada, b_ada, g_pre, w_in, w_conv, w_pool, pool_scale, w_out, g_post):
    B, T, D = x.shape
    c_act = jax.nn.silu(c)
    cw = CONV_WIDTH
    for l in range(DEPTH):
        mod = c_act @ w_ada[l] + b_ada[l]
        shift, scale, gate = jnp.split(mod, 3, axis=-1)
        h = rms_norm(x, g_pre[l]) * (1.0 + scale[:, None, :]) + shift[:, None, :]
        proj = h @ w_in[l]
        u_a = proj[..., 0 * cw:1 * cw]
        b_a = proj[..., 1 * cw:2 * cw]
        c_a = proj[..., 2 * cw:3 * cw]
        gate_a = proj[..., 3 * cw:4 * cw]
        o = 4 * cw
        u_p = proj[..., o:o + POOL_WIDTH]
        gate_p = proj[..., o + POOL_WIDTH:o + 2 * POOL_WIDTH]
        y_a = b_a * causal_depthwise_conv(c_a * u_a, w_conv[l]) * jax.nn.silu(gate_a)
        y_p = causal_multiscale_pool(u_p, w_pool[l], pool_scale[l]) * jax.nn.silu(gate_p)
        y = jnp.concatenate([y_a, y_p], axis=-1) @ w_out[l]
        x = x + gate[:, None, :] * rms_norm(y, g_post[l])
    return x
```

```python
import functools

import jax
import jax.numpy as jnp
from jax import lax
from jax.experimental import pallas as pl
from jax.experimental.pallas import tpu as pltpu

CONV_K = 3
POOL_WINDOWS = (2, 4, 8, 16)
NORM_EPS = 1e-6
HALO = 16
SEQ_TILE = 512
ADA_COL_TILE = 1024
VMEM_LIMIT_BYTES = 56 * 1024 * 1024


def _silu(v):
    return v * jax.nn.sigmoid(v)


def _ada_kernel(c_ref, w_ref, b_ref, o_ref):
    c_act = _silu(c_ref[...]).astype(jnp.bfloat16)
    w = w_ref[...].astype(jnp.bfloat16)
    o_ref[...] = jnp.dot(c_act, w, preferred_element_type=jnp.float32) + b_ref[...]


def _ada_modulation(c, w_ada, b_ada):
    depth, d, cols = w_ada.shape
    batch = c.shape[0]
    tn = ADA_COL_TILE
    return pl.pallas_call(
        _ada_kernel,
        grid=(depth, cols // tn),
        in_specs=[
            pl.BlockSpec((batch, d), lambda l, n: (0, 0)),
            pl.BlockSpec((None, d, tn), lambda l, n: (l, 0, n)),
            pl.BlockSpec((None, 1, tn), lambda l, n: (l, 0, n)),
        ],
        out_specs=pl.BlockSpec((None, batch, tn), lambda l, n: (l, 0, n)),
        out_shape=jax.ShapeDtypeStruct((depth, batch, cols), jnp.float32),
        compiler_params=pltpu.CompilerParams(
            dimension_semantics=("arbitrary", "arbitrary"),
            vmem_limit_bytes=VMEM_LIMIT_BYTES),
        name="ada_modulation",
    )(c, w_ada, b_ada.reshape(depth, 1, cols))


def _layer_kernel(x_ref, mod_ref, gpre_ref, gpost_ref, win_ref, wconv_ref, wpool_ref,
                  pscale_ref, wout_ref, o_ref, zext_ref, pext_ref, y_ref, *, cw, gd):
    j = pl.program_id(1)
    tt = x_ref.shape[0]

    @pl.when(j == 0)
    def _():
        zext_ref[0:HALO, :] = jnp.zeros((HALO, zext_ref.shape[1]), jnp.float32)
        pext_ref[0:HALO, :] = jnp.zeros((HALO, pext_ref.shape[1]), jnp.float32)

    x = x_ref[...]
    shift = mod_ref[0:1, :]
    scale = mod_ref[1:2, :]
    gate = mod_ref[2:3, :]

    r = lax.rsqrt(jnp.mean(x * x, axis=-1, keepdims=True) + NORM_EPS)
    h = (x * r) * (gpre_ref[...] * (1.0 + scale)) + shift
    hb = h.astype(jnp.bfloat16)

    def proj(k, width):
        return jnp.dot(hb, win_ref[:, k:k + width], preferred_element_type=jnp.float32)

    z = proj(2 * cw, cw) * proj(0, cw)
    zext_ref[HALO:HALO + tt, :] = z
    conv = wconv_ref[CONV_K - 1:CONV_K, :] * z
    for k in range(CONV_K - 1):
        lag = CONV_K - 1 - k
        conv = conv + wconv_ref[k:k + 1, :] * zext_ref[HALO - lag:HALO - lag + tt, :]
    zext_ref[0:HALO, :] = zext_ref[tt:tt + HALO, :]
    y_a = proj(cw, cw) * conv * _silu(proj(3 * cw, cw))
    y_ref[:, 0:cw] = y_a.astype(jnp.bfloat16)

    po = 4 * cw
    pw = gd * len(POOL_WINDOWS)
    u_p = proj(po, pw)
    pext_ref[HALO:HALO + tt, :] = u_p
    pos = (lax.broadcasted_iota(jnp.int32, (tt, gd), 0) + (j * tt + 1)).astype(jnp.float32)
    mixed = []
    for g, w in enumerate(POOL_WINDOWS):
        cols = slice(g * gd, (g + 1) * gd)
        pg = u_p[:, cols]
        acc = pg
        for lag in range(1, w):
            acc = acc + pext_ref[HALO - lag:HALO - lag + tt, cols]
        pooled = acc / jnp.minimum(pos, float(w)) - pg
        mixed.append(jnp.dot(pooled.astype(jnp.bfloat16), wpool_ref[g],
                             preferred_element_type=jnp.float32))
    pext_ref[0:HALO, :] = pext_ref[tt:tt + HALO, :]
    y_p = jnp.concatenate(mixed, axis=-1) * pscale_ref[...] * _silu(proj(po + pw, pw))
    y_ref[:, cw:cw + pw] = y_p.astype(jnp.bfloat16)

    yo = jnp.dot(y_ref[...], wout_ref[...], preferred_element_type=jnp.float32)
    r2 = lax.rsqrt(jnp.mean(yo * yo, axis=-1, keepdims=True) + NORM_EPS)
    o_ref[...] = x + gate * ((yo * r2) * gpost_ref[...])


def _layer(x, mod, g_pre, g_post, w_in, w_conv, w_pool, pool_scale, w_out):
    batch, seq, d = x.shape
    cw = w_conv.shape[1]
    groups, gd, _ = w_pool.shape
    pw = groups * gd
    tt = SEQ_TILE
    const = lambda b, j: (0, 0)
    return pl.pallas_call(
        functools.partial(_layer_kernel, cw=cw, gd=gd),
        grid=(batch, seq // tt),
        in_specs=[
            pl.BlockSpec((None, tt, d), lambda b, j: (b, j, 0)),
            pl.BlockSpec((None, 3, d), lambda b, j: (b, 0, 0)),
            pl.BlockSpec((1, d), const),
            pl.BlockSpec((1, d), const),
            pl.BlockSpec(w_in.shape, const),
            pl.BlockSpec(w_conv.shape, const),
            pl.BlockSpec(w_pool.shape, lambda b, j: (0, 0, 0)),
            pl.BlockSpec((1, pw), const),
            pl.BlockSpec(w_out.shape, const),
        ],
        out_specs=pl.BlockSpec((None, tt, d), lambda b, j: (b, j, 0)),
        out_shape=jax.ShapeDtypeStruct(x.shape, x.dtype),
        scratch_shapes=[
            pltpu.VMEM((HALO + tt, cw), jnp.float32),
            pltpu.VMEM((HALO + tt, pw), jnp.float32),
            pltpu.VMEM((tt, cw + pw), jnp.bfloat16),
        ],
        compiler_params=pltpu.CompilerParams(
            dimension_semantics=("arbitrary", "arbitrary"),
            vmem_limit_bytes=VMEM_LIMIT_BYTES),
        name="mixer_layer",
    )(x, mod, g_pre.reshape(1, d), g_post.reshape(1, d), w_in, w_conv, w_pool,
      pool_scale.reshape(1, pw), w_out)


def kernel(x, c, w_ada, b_ada, g_pre, w_in, w_conv, w_pool, pool_scale, w_out, g_post):
    depth = w_ada.shape[0]
    batch, _, d = x.shape
    mod = _ada_modulation(c, w_ada, b_ada).reshape(depth, batch, 3, d)
    w_in_b = w_in.astype(jnp.bfloat16)
    w_out_b = w_out.astype(jnp.bfloat16)
    w_pool_b = w_pool.astype(jnp.bfloat16)
    for l in range(depth):
        x = _layer(x, mod[l], g_pre[l], g_post[l], w_in_b[l], w_conv[l], w_pool_b[l],
                   pool_scale[l], w_out_b[l])
    return x
```

```python
import functools

import jax
import jax.numpy as jnp
from jax import lax
from jax.experimental import pallas as pl
from jax.experimental.pallas import tpu as pltpu

CONV_K = 3
POOL_WINDOWS = (2, 4, 8, 16)
NORM_EPS = 1e-6
HALO = 16
SEQ_TILE = 1024
SUB_ROWS = 256
ADA_COL_TILE = 1024
VMEM_LIMIT_BYTES = 56 * 1024 * 1024


def _silu(v):
    return v * jax.nn.sigmoid(v)


def _ada_kernel(c_ref, w_ref, b_ref, o_ref):
    c_act = _silu(c_ref[...]).astype(jnp.bfloat16)
    w = w_ref[...].astype(jnp.bfloat16)
    o_ref[...] = jnp.dot(c_act, w, preferred_element_type=jnp.float32) + b_ref[...]


def _ada_modulation(c, w_ada, b_ada):
    depth, d, cols = w_ada.shape
    batch = c.shape[0]
    tn = ADA_COL_TILE
    return pl.pallas_call(
        _ada_kernel,
        grid=(depth, cols // tn),
        in_specs=[
            pl.BlockSpec((batch, d), lambda l, n: (0, 0)),
            pl.BlockSpec((None, d, tn), lambda l, n: (l, 0, n)),
            pl.BlockSpec((None, 1, tn), lambda l, n: (l, 0, n)),
        ],
        out_specs=pl.BlockSpec((None, batch, tn), lambda l, n: (l, 0, n)),
        out_shape=jax.ShapeDtypeStruct((depth, batch, cols), jnp.float32),
        compiler_params=pltpu.CompilerParams(
            dimension_semantics=("arbitrary", "arbitrary"),
            vmem_limit_bytes=VMEM_LIMIT_BYTES),
        name="ada_modulation",
    )(c, w_ada, b_ada.reshape(depth, 1, cols))


def _layer_kernel(x_ref, mod_ref, gpre_ref, gpost_ref, win_ref, wconv_ref, wpool_ref,
                  pscale_ref, wout_ref, o_ref, zext_ref, pext_ref, h_ref, y_ref, *, cw, gd, sub):
    j = pl.program_id(1)
    tt = x_ref.shape[0]
    pw = gd * len(POOL_WINDOWS)
    po = 4 * cw

    @pl.when(j == 0)
    def _():
        zext_ref[0:HALO, :] = jnp.zeros((HALO, cw), jnp.float32)
        pext_ref[0:HALO, :] = jnp.zeros((HALO, pw), jnp.float32)

    shift = mod_ref[0:1, :]
    gain = gpre_ref[...] * (1.0 + mod_ref[1:2, :])
    out_gain = mod_ref[2:3, :] * gpost_ref[...]

    def norm(r0):
        x = x_ref[r0:r0 + sub, :]
        r = lax.rsqrt(jnp.mean(x * x, axis=-1, keepdims=True) + NORM_EPS)
        h_ref[r0:r0 + sub, :] = ((x * r) * gain + shift).astype(jnp.bfloat16)

    def project(r0):
        return [jnp.dot(h_ref[r0:r0 + sub, :], win_ref[:, k:k + cw], preferred_element_type=jnp.float32)
                for k in range(0, po + 2 * pw, cw)]

    def mix(r0, projs):
        rows = slice(r0, r0 + sub)
        ext = slice(HALO + r0, HALO + r0 + sub)
        u_a, b_a, c_a, gate_a, u_p, gate_p = projs

        z = c_a * u_a
        zext_ref[ext, :] = z
        conv = wconv_ref[CONV_K - 1:CONV_K, :] * z
        for k in range(CONV_K - 1):
            lag = CONV_K - 1 - k
            conv = conv + wconv_ref[k:k + 1, :] * zext_ref[HALO + r0 - lag:HALO + r0 - lag + sub, :]
        y_ref[rows, 0:cw] = (b_a * conv * _silu(gate_a)).astype(jnp.bfloat16)

        pext_ref[ext, :] = u_p
        pos = (lax.broadcasted_iota(jnp.int32, (sub, gd), 0) + (j * tt + r0 + 1)).astype(jnp.float32)
        mixed = []
        for g, w in enumerate(POOL_WINDOWS):
            cols = slice(g * gd, (g + 1) * gd)
            pg = u_p[:, cols]
            acc = pg
            for lag in range(1, w):
                acc = acc + pext_ref[HALO + r0 - lag:HALO + r0 - lag + sub, cols]
            pooled = acc / jnp.minimum(pos, float(w)) - pg
            mixed.append(jnp.dot(pooled.astype(jnp.bfloat16), wpool_ref[g],
                                 preferred_element_type=jnp.float32))
        y_p = jnp.concatenate(mixed, axis=-1) * pscale_ref[...] * _silu(gate_p)
        y_ref[rows, cw:cw + pw] = y_p.astype(jnp.bfloat16)

        yo = jnp.dot(y_ref[rows, :], wout_ref[...], preferred_element_type=jnp.float32)
        r2 = lax.rsqrt(jnp.mean(yo * yo, axis=-1, keepdims=True) + NORM_EPS)
        o_ref[rows, :] = x_ref[rows, :] + (yo * r2) * out_gain

    norm(0)
    projs = project(0)
    if sub < tt:
        norm(sub)
    for r0 in range(0, tt, sub):
        nxt = project(r0 + sub) if r0 + sub < tt else None
        if r0 + 2 * sub < tt:
            norm(r0 + 2 * sub)
        mix(r0, projs)
        projs = nxt

    zext_ref[0:HALO, :] = zext_ref[tt:tt + HALO, :]
    pext_ref[0:HALO, :] = pext_ref[tt:tt + HALO, :]


def _layer(l, x, mod, g_pre, g_post, w_in, w_conv, w_pool, pool_scale, w_out):
    batch, seq, d = x.shape
    cw = w_conv.shape[2]
    _, groups, gd, _ = w_pool.shape
    pw = groups * gd
    tt = SEQ_TILE
    per_layer = lambda b, j: (l, 0, 0)
    return pl.pallas_call(
        functools.partial(_layer_kernel, cw=cw, gd=gd, sub=SUB_ROWS),
        grid=(batch, seq // tt),
        in_specs=[
            pl.BlockSpec((None, tt, d), lambda b, j: (b, j, 0)),
            pl.BlockSpec((None, None, 3, d), lambda b, j: (l, b, 0, 0)),
            pl.BlockSpec((None, 1, d), per_layer),
            pl.BlockSpec((None, 1, d), per_layer),
            pl.BlockSpec((None,) + w_in.shape[1:], per_layer),
            pl.BlockSpec((None,) + w_conv.shape[1:], per_layer),
            pl.BlockSpec((None,) + w_pool.shape[1:], lambda b, j: (l, 0, 0, 0)),
            pl.BlockSpec((None, 1, pw), per_layer),
            pl.BlockSpec((None,) + w_out.shape[1:], per_layer),
        ],
        out_specs=pl.BlockSpec((None, tt, d), lambda b, j: (b, j, 0)),
        out_shape=jax.ShapeDtypeStruct(x.shape, x.dtype),
        scratch_shapes=[
            pltpu.VMEM((HALO + tt, cw), jnp.float32),
            pltpu.VMEM((HALO + tt, pw), jnp.float32),
            pltpu.VMEM((tt, d), jnp.bfloat16),
            pltpu.VMEM((tt, cw + pw), jnp.bfloat16),
        ],
        compiler_params=pltpu.CompilerParams(
            dimension_semantics=("arbitrary", "arbitrary"),
            vmem_limit_bytes=VMEM_LIMIT_BYTES),
        name="mixer_layer",
    )(x, mod, g_pre, g_post, w_in, w_conv, w_pool, pool_scale, w_out)


def kernel(x, c, w_ada, b_ada, g_pre, w_in, w_conv, w_pool, pool_scale, w_out, g_post):
    depth = w_ada.shape[0]
    batch, _, d = x.shape
    mod = _ada_modulation(c, w_ada, b_ada).reshape(depth, batch, 3, d)
    w_in_b = w_in.astype(jnp.bfloat16)
    w_out_b = w_out.astype(jnp.bfloat16)
    w_pool_b = w_pool.astype(jnp.bfloat16)
    g_pre3 = g_pre.reshape(depth, 1, d)
    g_post3 = g_post.reshape(depth, 1, d)
    pool_scale3 = pool_scale.reshape(depth, 1, -1)
    for l in range(depth):
        x = _layer(l, x, mod, g_pre3, g_post3, w_in_b, w_conv, w_pool_b, pool_scale3, w_out_b)
    return x
```

```python
import functools

import jax
import jax.numpy as jnp
from jax import lax
from jax.experimental import pallas as pl
from jax.experimental.pallas import tpu as pltpu

CONV_K = 3
POOL_WINDOWS = (2, 4, 8, 16)
NORM_EPS = 1e-6
HALO = 16
SEQ_TILE = 1024
SUB_ROWS = 256
ADA_COL_TILE = 1024
VMEM_LIMIT_BYTES = 56 * 1024 * 1024


def _silu(v):
    return v * jax.nn.sigmoid(v)


def _ada_kernel(c_ref, w_ref, b_ref, o_ref):
    c_act = _silu(c_ref[...]).astype(jnp.bfloat16)
    w = w_ref[...].astype(jnp.bfloat16)
    o_ref[...] = jnp.dot(c_act, w, preferred_element_type=jnp.float32) + b_ref[...]


def _ada_modulation(c, w_ada, b_ada):
    depth, d, cols = w_ada.shape
    batch = c.shape[0]
    tn = ADA_COL_TILE
    return pl.pallas_call(
        _ada_kernel,
        grid=(depth, cols // tn),
        in_specs=[
            pl.BlockSpec((batch, d), lambda l, n: (0, 0)),
            pl.BlockSpec((None, d, tn), lambda l, n: (l, 0, n)),
            pl.BlockSpec((None, 1, tn), lambda l, n: (l, 0, n)),
        ],
        out_specs=pl.BlockSpec((None, batch, tn), lambda l, n: (l, 0, n)),
        out_shape=jax.ShapeDtypeStruct((depth, batch, cols), jnp.float32),
        compiler_params=pltpu.CompilerParams(
            dimension_semantics=("arbitrary", "arbitrary"),
            vmem_limit_bytes=VMEM_LIMIT_BYTES),
        name="ada_modulation",
    )(c, w_ada, b_ada.reshape(depth, 1, cols))


def _layer_kernel(x_ref, mod_ref, gpre_ref, gpost_ref, win_ref, wconv_ref, wpool_ref,
                  pscale_ref, wout_ref, o_ref, zc_ref, pc_ref, wpair_ref, h_ref, y_ref, *, cw, gd, sub):
    j = pl.program_id(1)
    tt = x_ref.shape[0]
    pw = gd * len(POOL_WINDOWS)
    po = 4 * cw

    @pl.when(j == 0)
    def _():
        zc_ref[...] = jnp.zeros((HALO, cw), jnp.float32)
        pc_ref[...] = jnp.zeros((HALO, pw), jnp.float32)

    zero = jnp.zeros((gd, gd), jnp.bfloat16)
    for g in range(0, len(POOL_WINDOWS), 2):
        wpair_ref[g // 2] = jnp.concatenate(
            [jnp.concatenate([wpool_ref[g], zero], axis=-1),
             jnp.concatenate([zero, wpool_ref[g + 1]], axis=-1)], axis=0)

    shift = mod_ref[0:1, :]
    gain = gpre_ref[...] * (1.0 + mod_ref[1:2, :])
    out_gain = mod_ref[2:3, :] * gpost_ref[...]

    def norm(r0):
        x = x_ref[r0:r0 + sub, :]
        r = lax.rsqrt(jnp.mean(x * x, axis=-1, keepdims=True) + NORM_EPS)
        h_ref[r0:r0 + sub, :] = ((x * r) * gain + shift).astype(jnp.bfloat16)

    def project(r0, first, last):
        return [jnp.dot(h_ref[r0:r0 + sub, :], win_ref[:, k * cw:(k + 1) * cw],
                        preferred_element_type=jnp.float32) for k in range(first, last)]

    def lagged(e, lag):
        return pltpu.roll(e, lag, axis=0)[HALO:, :]

    def mix_conv(r0, projs):
        u_a, b_a, c_a, gate_a = projs
        z = c_a * u_a
        ez = jnp.concatenate([zc_ref[...], z], axis=0)
        zc_ref[...] = z[sub - HALO:, :]
        conv = wconv_ref[CONV_K - 1:CONV_K, :] * z
        for k in range(CONV_K - 1):
            conv = conv + wconv_ref[k:k + 1, :] * lagged(ez, CONV_K - 1 - k)
        y_ref[r0:r0 + sub, 0:cw] = (b_a * conv * _silu(gate_a)).astype(jnp.bfloat16)

    def mix_pool(r0, u_p):
        ep = jnp.concatenate([pc_ref[...], u_p], axis=0)
        pc_ref[...] = u_p[sub - HALO:, :]
        pooled = []
        for g, w in enumerate(POOL_WINDOWS):
            cols = slice(g * gd, (g + 1) * gd)
            s = ep[:, cols]
            span = 1
            while span < min(w, 8):
                s = s + pltpu.roll(s, span, axis=0)
                span *= 2
            if span < w:
                s = s[8:, :] + s[:-8, :]
                s = s[HALO - 8:, :]
            else:
                s = s[HALO:, :]
            if r0 == 0:
                pos = (lax.broadcasted_iota(jnp.int32, (HALO, gd), 0) + (j * tt + 1)).astype(jnp.float32)
                inv = jnp.concatenate([1.0 / jnp.minimum(pos, float(w)),
                                       jnp.full((sub - HALO, gd), 1.0 / w, jnp.float32)], axis=0)
            else:
                inv = 1.0 / w
            pooled.append(s * inv - u_p[:, cols])
        mixed = []
        for g in range(0, len(POOL_WINDOWS), 2):
            pair = jnp.concatenate(pooled[g:g + 2], axis=-1).astype(jnp.bfloat16)
            mixed.append(jnp.dot(pair, wpair_ref[g // 2], preferred_element_type=jnp.float32))
        return jnp.concatenate(mixed, axis=-1)

    def finish(r0, mixed, gate_p):
        rows = slice(r0, r0 + sub)
        y_ref[rows, cw:cw + pw] = (mixed * pscale_ref[...] * _silu(gate_p)).astype(jnp.bfloat16)
        yo = jnp.dot(y_ref[rows, :], wout_ref[...], preferred_element_type=jnp.float32)
        r2 = lax.rsqrt(jnp.mean(yo * yo, axis=-1, keepdims=True) + NORM_EPS)
        o_ref[rows, :] = x_ref[rows, :] + (yo * r2) * out_gain

    n_conv = po // cw
    n_proj = n_conv + 2 * pw // cw
    norm(0)
    conv_projs = project(0, 0, n_conv)
    pool_projs = project(0, n_conv, n_proj)
    if sub < tt:
        norm(sub)
    for r0 in range(0, tt, sub):
        more = r0 + sub < tt
        if r0 + 2 * sub < tt:
            norm(r0 + 2 * sub)
        next_conv = project(r0 + sub, 0, n_conv) if more else None
        mix_conv(r0, conv_projs)
        mixed = mix_pool(r0, pool_projs[0])
        next_pool = project(r0 + sub, n_conv, n_proj) if more else None
        finish(r0, mixed, pool_projs[1])
        conv_projs, pool_projs = next_conv, next_pool


def _layer(l, x, mod, g_pre, g_post, w_in, w_conv, w_pool, pool_scale, w_out):
    batch, seq, d = x.shape
    cw = w_conv.shape[2]
    _, groups, gd, _ = w_pool.shape
    pw = groups * gd
    tt = SEQ_TILE
    per_layer = lambda b, j: (l, 0, 0)
    return pl.pallas_call(
        functools.partial(_layer_kernel, cw=cw, gd=gd, sub=SUB_ROWS),
        grid=(batch, seq // tt),
        in_specs=[
            pl.BlockSpec((None, tt, d), lambda b, j: (b, j, 0)),
            pl.BlockSpec((None, None, 3, d), lambda b, j: (l, b, 0, 0)),
            pl.BlockSpec((None, 1, d), per_layer),
            pl.BlockSpec((None, 1, d), per_layer),
            pl.BlockSpec((None,) + w_in.shape[1:], per_layer),
            pl.BlockSpec((None,) + w_conv.shape[1:], per_layer),
            pl.BlockSpec((None,) + w_pool.shape[1:], lambda b, j: (l, 0, 0, 0)),
            pl.BlockSpec((None, 1, pw), per_layer),
            pl.BlockSpec((None,) + w_out.shape[1:], per_layer),
        ],
        out_specs=pl.BlockSpec((None, tt, d), lambda b, j: (b, j, 0)),
        out_shape=jax.ShapeDtypeStruct(x.shape, x.dtype),
        scratch_shapes=[
            pltpu.VMEM((HALO, cw), jnp.float32),
            pltpu.VMEM((HALO, pw), jnp.float32),
            pltpu.VMEM((groups // 2, 2 * gd, 2 * gd), jnp.bfloat16),
            pltpu.VMEM((tt, d), jnp.bfloat16),
            pltpu.VMEM((tt, cw + pw), jnp.bfloat16),
        ],
        compiler_params=pltpu.CompilerParams(
            dimension_semantics=("arbitrary", "arbitrary"),
            vmem_limit_bytes=VMEM_LIMIT_BYTES),
        name="mixer_layer",
    )(x, mod, g_pre, g_post, w_in, w_conv, w_pool, pool_scale, w_out)


def kernel(x, c, w_ada, b_ada, g_pre, w_in, w_conv, w_pool, pool_scale, w_out, g_post):
    depth = w_ada.shape[0]
    batch, _, d = x.shape
    mod = _ada_modulation(c, w_ada, b_ada).reshape(depth, batch, 3, d)
    w_in_b = w_in.astype(jnp.bfloat16)
    w_out_b = w_out.astype(jnp.bfloat16)
    w_pool_b = w_pool.astype(jnp.bfloat16)
    g_pre3 = g_pre.reshape(depth, 1, d)
    g_post3 = g_post.reshape(depth, 1, d)
    pool_scale3 = pool_scale.reshape(depth, 1, -1)
    for l in range(depth):
        x = _layer(l, x, mod, g_pre3, g_post3, w_in_b, w_conv, w_pool_b, pool_scale3, w_out_b)
    return x
```

```python
import functools

import jax
import jax.numpy as jnp
from jax import lax
from jax.experimental import pallas as pl
from jax.experimental.pallas import tpu as pltpu

CONV_K = 3
POOL_WINDOWS = (2, 4, 8, 16)
NORM_EPS = 1e-6
HALO = 16
SEQ_TILE = 1024
SUB_ROWS = 256
ADA_COL_TILE = 1024
WEIGHT_CAST_ROWS = 128
VMEM_LIMIT_BYTES = 56 * 1024 * 1024


def _silu(v):
    return v * jax.nn.sigmoid(v)


def _ada_kernel(c_ref, w_ref, b_ref, o_ref):
    c_act = _silu(c_ref[...]).astype(jnp.bfloat16)
    w = w_ref[...].astype(jnp.bfloat16)
    o_ref[...] = jnp.dot(c_act, w, preferred_element_type=jnp.float32) + b_ref[...]


def _ada_modulation(c, w_ada, b_ada):
    depth, d, cols = w_ada.shape
    batch = c.shape[0]
    tn = ADA_COL_TILE
    return pl.pallas_call(
        _ada_kernel,
        grid=(depth, cols // tn),
        in_specs=[
            pl.BlockSpec((batch, d), lambda l, n: (0, 0)),
            pl.BlockSpec((None, d, tn), lambda l, n: (l, 0, n)),
            pl.BlockSpec((None, 1, tn), lambda l, n: (l, 0, n)),
        ],
        out_specs=pl.BlockSpec((None, batch, tn), lambda l, n: (l, 0, n)),
        out_shape=jax.ShapeDtypeStruct((depth, batch, cols), jnp.float32),
        compiler_params=pltpu.CompilerParams(
            dimension_semantics=("arbitrary", "arbitrary"),
            vmem_limit_bytes=VMEM_LIMIT_BYTES),
        name="ada_modulation",
    )(c, w_ada, b_ada.reshape(depth, 1, cols))


def _layer_kernel(x_ref, mod_ref, gpre_ref, gpost_ref, win32_ref, wconv_ref, wpool_ref,
                  pscale_ref, wout32_ref, o_ref, win_ref, wout_ref, zc_ref, pc_ref, wpair_ref,
                  h_ref, y_ref, *, cw, gd, sub):
    j = pl.program_id(1)
    tt = x_ref.shape[0]
    pw = gd * len(POOL_WINDOWS)
    po = 4 * cw

    @pl.when(jnp.logical_and(pl.program_id(0) == 0, j == 0))
    def _():
        def cast_rows(i, carry):
            rows = pl.ds(pl.multiple_of(i * WEIGHT_CAST_ROWS, WEIGHT_CAST_ROWS), WEIGHT_CAST_ROWS)
            win_ref[rows, :] = win32_ref[rows, :].astype(jnp.bfloat16)
            wout_ref[rows, :] = wout32_ref[rows, :].astype(jnp.bfloat16)
            return carry
        lax.fori_loop(0, win_ref.shape[0] // WEIGHT_CAST_ROWS, cast_rows, 0)
        zero = jnp.zeros((gd, gd), jnp.bfloat16)
        for g in range(0, len(POOL_WINDOWS), 2):
            wpair_ref[g // 2] = jnp.concatenate(
                [jnp.concatenate([wpool_ref[g].astype(jnp.bfloat16), zero], axis=-1),
                 jnp.concatenate([zero, wpool_ref[g + 1].astype(jnp.bfloat16)], axis=-1)], axis=0)

    @pl.when(j == 0)
    def _():
        zc_ref[...] = jnp.zeros((HALO, cw), jnp.float32)
        pc_ref[...] = jnp.zeros((HALO, pw), jnp.float32)

    shift = mod_ref[0:1, :]
    gain = gpre_ref[...] * (1.0 + mod_ref[1:2, :])
    out_gain = mod_ref[2:3, :] * gpost_ref[...]

    def norm(r0):
        x = x_ref[r0:r0 + sub, :]
        r = lax.rsqrt(jnp.mean(x * x, axis=-1, keepdims=True) + NORM_EPS)
        h_ref[r0:r0 + sub, :] = ((x * r) * gain + shift).astype(jnp.bfloat16)

    def project(r0, first, last):
        return [jnp.dot(h_ref[r0:r0 + sub, :], win_ref[:, k * cw:(k + 1) * cw],
                        preferred_element_type=jnp.float32) for k in range(first, last)]

    def lagged(e, lag):
        return pltpu.roll(e, lag, axis=0)[HALO:, :]

    def mix_conv(r0, projs):
        u_a, b_a, c_a, gate_a = projs
        z = c_a * u_a
        ez = jnp.concatenate([zc_ref[...], z], axis=0)
        zc_ref[...] = z[sub - HALO:, :]
        conv = wconv_ref[CONV_K - 1:CONV_K, :] * z
        for k in range(CONV_K - 1):
            conv = conv + wconv_ref[k:k + 1, :] * lagged(ez, CONV_K - 1 - k)
        y_ref[r0:r0 + sub, 0:cw] = (b_a * conv * _silu(gate_a)).astype(jnp.bfloat16)

    def mix_pool(r0, u_p):
        ep = jnp.concatenate([pc_ref[...], u_p], axis=0)
        pc_ref[...] = u_p[sub - HALO:, :]
        pooled = []
        for g, w in enumerate(POOL_WINDOWS):
            cols = slice(g * gd, (g + 1) * gd)
            s = ep[:, cols]
            span = 1
            while span < min(w, 8):
                s = s + pltpu.roll(s, span, axis=0)
                span *= 2
            if span < w:
                s = s[8:, :] + s[:-8, :]
                s = s[HALO - 8:, :]
            else:
                s = s[HALO:, :]
            if r0 == 0:
                pos = (lax.broadcasted_iota(jnp.int32, (HALO, gd), 0) + (j * tt + 1)).astype(jnp.float32)
                inv = jnp.concatenate([1.0 / jnp.minimum(pos, float(w)),
                                       jnp.full((sub - HALO, gd), 1.0 / w, jnp.float32)], axis=0)
            else:
                inv = 1.0 / w
            pooled.append(s * inv - u_p[:, cols])
        mixed = []
        for g in range(0, len(POOL_WINDOWS), 2):
            pair = jnp.concatenate(pooled[g:g + 2], axis=-1).astype(jnp.bfloat16)
            mixed.append(jnp.dot(pair, wpair_ref[g // 2], preferred_element_type=jnp.float32))
        return jnp.concatenate(mixed, axis=-1)

    def finish(r0, mixed, gate_p):
        rows = slice(r0, r0 + sub)
        y_ref[rows, cw:cw + pw] = (mixed * pscale_ref[...] * _silu(gate_p)).astype(jnp.bfloat16)
        yo = jnp.dot(y_ref[rows, :], wout_ref[...], preferred_element_type=jnp.float32)
        r2 = lax.rsqrt(jnp.mean(yo * yo, axis=-1, keepdims=True) + NORM_EPS)
        o_ref[rows, :] = x_ref[rows, :] + (yo * r2) * out_gain

    n_conv = po // cw
    n_proj = n_conv + 2 * pw // cw
    norm(0)
    conv_projs = project(0, 0, n_conv)
    pool_projs = project(0, n_conv, n_proj)
    if sub < tt:
        norm(sub)
    for r0 in range(0, tt, sub):
        more = r0 + sub < tt
        if r0 + 2 * sub < tt:
            norm(r0 + 2 * sub)
        next_conv = project(r0 + sub, 0, n_conv) if more else None
        mix_conv(r0, conv_projs)
        mixed = mix_pool(r0, pool_projs[0])
        next_pool = project(r0 + sub, n_conv, n_proj) if more else None
        finish(r0, mixed, pool_projs[1])
        conv_projs, pool_projs = next_conv, next_pool


def _layer(l, x, mod, g_pre, g_post, w_in, w_conv, w_pool, pool_scale, w_out):
    batch, seq, d = x.shape
    cw = w_conv.shape[2]
    _, groups, gd, _ = w_pool.shape
    pw = groups * gd
    tt = SEQ_TILE
    per_layer = lambda b, j: (l, 0, 0)
    assert w_in.shape[1] == w_out.shape[1] and w_in.shape[1] % WEIGHT_CAST_ROWS == 0
    once = pl.Buffered(1)
    return pl.pallas_call(
        functools.partial(_layer_kernel, cw=cw, gd=gd, sub=SUB_ROWS),
        grid=(batch, seq // tt),
        in_specs=[
            pl.BlockSpec((None, tt, d), lambda b, j: (b, j, 0)),
            pl.BlockSpec((None, None, 3, d), lambda b, j: (l, b, 0, 0)),
            pl.BlockSpec((None, 1, d), per_layer),
            pl.BlockSpec((None, 1, d), per_layer),
            pl.BlockSpec((None,) + w_in.shape[1:], per_layer, pipeline_mode=once),
            pl.BlockSpec((None,) + w_conv.shape[1:], per_layer),
            pl.BlockSpec((None,) + w_pool.shape[1:], lambda b, j: (l, 0, 0, 0)),
            pl.BlockSpec((None, 1, pw), per_layer),
            pl.BlockSpec((None,) + w_out.shape[1:], per_layer, pipeline_mode=once),
        ],
        out_specs=pl.BlockSpec((None, tt, d), lambda b, j: (b, j, 0)),
        out_shape=jax.ShapeDtypeStruct(x.shape, x.dtype),
        scratch_shapes=[
            pltpu.VMEM(w_in.shape[1:], jnp.bfloat16),
            pltpu.VMEM(w_out.shape[1:], jnp.bfloat16),
            pltpu.VMEM((HALO, cw), jnp.float32),
            pltpu.VMEM((HALO, pw), jnp.float32),
            pltpu.VMEM((groups // 2, 2 * gd, 2 * gd), jnp.bfloat16),
            pltpu.VMEM((tt, d), jnp.bfloat16),
            pltpu.VMEM((tt, cw + pw), jnp.bfloat16),
        ],
        compiler_params=pltpu.CompilerParams(
            dimension_semantics=("arbitrary", "arbitrary"),
            vmem_limit_bytes=VMEM_LIMIT_BYTES),
        name="mixer_layer",
    )(x, mod, g_pre, g_post, w_in, w_conv, w_pool, pool_scale, w_out)


def kernel(x, c, w_ada, b_ada, g_pre, w_in, w_conv, w_pool, pool_scale, w_out, g_post):
    depth = w_ada.shape[0]
    batch, _, d = x.shape
    mod = _ada_modulation(c, w_ada, b_ada).reshape(depth, batch, 3, d)
    g_pre3 = g_pre.reshape(depth, 1, d)
    g_post3 = g_post.reshape(depth, 1, d)
    pool_scale3 = pool_scale.reshape(depth, 1, -1)
    for l in range(depth):
        x = _layer(l, x, mod, g_pre3, g_post3, w_in, w_conv, w_pool, pool_scale3, w_out)
    return x
```

```python
import functools

import jax
import jax.numpy as jnp
from jax import lax
from jax.experimental import pallas as pl
from jax.experimental.pallas import tpu as pltpu

CONV_K = 3
POOL_WINDOWS = (2, 4, 8, 16)
NORM_EPS = 1e-6
HALO = 16
SEQ_TILE = 1024
SUB_ROWS = 256
ADA_COL_TILE = 1024
WEIGHT_CAST_ROWS = 128
VMEM_LIMIT_BYTES = 56 * 1024 * 1024


def _silu(v):
    return v * jax.nn.sigmoid(v)


def _ada_kernel(c_ref, w_ref, b_ref, o_ref):
    c_act = _silu(c_ref[...]).astype(jnp.bfloat16)
    w = w_ref[...].astype(jnp.bfloat16)
    o_ref[...] = jnp.dot(c_act, w, preferred_element_type=jnp.float32) + b_ref[...]


def _ada_modulation(c, w_ada, b_ada):
    depth, d, cols = w_ada.shape
    batch = c.shape[0]
    tn = ADA_COL_TILE
    return pl.pallas_call(
        _ada_kernel,
        grid=(depth, cols // tn),
        in_specs=[
            pl.BlockSpec((batch, d), lambda l, n: (0, 0)),
            pl.BlockSpec((None, d, tn), lambda l, n: (l, 0, n)),
            pl.BlockSpec((None, 1, tn), lambda l, n: (l, 0, n)),
        ],
        out_specs=pl.BlockSpec((None, batch, tn), lambda l, n: (l, 0, n)),
        out_shape=jax.ShapeDtypeStruct((depth, batch, cols), jnp.float32),
        compiler_params=pltpu.CompilerParams(
            dimension_semantics=("arbitrary", "arbitrary"),
            vmem_limit_bytes=VMEM_LIMIT_BYTES),
        name="ada_modulation",
    )(c, w_ada, b_ada.reshape(depth, 1, cols))


def _layer_kernel(x_ref, mod_ref, gpre_ref, gpost_ref, win32_ref, wconv_ref, wpool_ref,
                  pscale_ref, wout32_ref, o_ref, win_ref, wout_ref, zc_ref, pc_ref, wpair_ref,
                  h_ref, y_ref, *, cw, gd, sub):
    j = pl.program_id(1)
    tt = x_ref.shape[0]
    pw = gd * len(POOL_WINDOWS)
    po = 4 * cw

    @pl.when(jnp.logical_and(pl.program_id(0) == 0, j == 0))
    def _():
        def cast_rows(i, carry):
            rows = pl.ds(pl.multiple_of(i * WEIGHT_CAST_ROWS, WEIGHT_CAST_ROWS), WEIGHT_CAST_ROWS)
            win_ref[rows, :] = win32_ref[rows, :].astype(jnp.bfloat16)
            wout_ref[rows, :] = wout32_ref[rows, :].astype(jnp.bfloat16)
            return carry
        lax.fori_loop(0, win_ref.shape[0] // WEIGHT_CAST_ROWS, cast_rows, 0)
        zero = jnp.zeros((gd, gd), jnp.bfloat16)
        for g in range(0, len(POOL_WINDOWS), 2):
            wpair_ref[g // 2] = jnp.concatenate(
                [jnp.concatenate([wpool_ref[g].astype(jnp.bfloat16), zero], axis=-1),
                 jnp.concatenate([zero, wpool_ref[g + 1].astype(jnp.bfloat16)], axis=-1)], axis=0)

    @pl.when(j == 0)
    def _():
        zc_ref[...] = jnp.zeros((HALO, cw), jnp.float32)
        pc_ref[...] = jnp.zeros((HALO, pw), jnp.float32)

    shift = mod_ref[0:1, :]
    gain = gpre_ref[...] * (1.0 + mod_ref[1:2, :])
    out_gain = mod_ref[2:3, :] * gpost_ref[...]

    def norm(r0):
        x = x_ref[r0:r0 + sub, :]
        r = lax.rsqrt(jnp.mean(x * x, axis=-1, keepdims=True) + NORM_EPS)
        hb = ((x * r) * gain + shift).astype(jnp.bfloat16)
        h_ref[r0:r0 + sub, :] = hb
        bits = pltpu.bitcast(hb, jnp.uint32)
        zeros = pltpu.bitcast((bits >> 16) >> 16, jnp.float32)
        return jnp.concatenate([zeros[:, k:k + cw] for k in range(0, zeros.shape[1], cw)], axis=0)

    def project(r0, first, last):
        return [jnp.dot(h_ref[r0:r0 + sub, :], win_ref[:, k * cw:(k + 1) * cw],
                        preferred_element_type=jnp.float32) for k in range(first, last)]

    def lagged(e, lag):
        return pltpu.roll(e, lag, axis=0)[HALO:, :]

    def mix_conv(r0, projs, ahead):
        u_a, b_a, c_a, gate_a = projs
        z = c_a * u_a
        if ahead is not None:
            z = z + ahead
        ez = jnp.concatenate([zc_ref[...], z], axis=0)
        zc_ref[...] = z[sub - HALO:, :]
        conv = wconv_ref[CONV_K - 1:CONV_K, :] * z
        for k in range(CONV_K - 1):
            conv = conv + wconv_ref[k:k + 1, :] * lagged(ez, CONV_K - 1 - k)
        y_ref[r0:r0 + sub, 0:cw] = (b_a * conv * _silu(gate_a)).astype(jnp.bfloat16)

    def mix_pool(r0, u_p):
        ep = jnp.concatenate([pc_ref[...], u_p], axis=0)
        pc_ref[...] = u_p[sub - HALO:, :]
        pooled = []
        for g, w in enumerate(POOL_WINDOWS):
            cols = slice(g * gd, (g + 1) * gd)
            s = ep[:, cols]
            span = 1
            while span < min(w, 8):
                s = s + pltpu.roll(s, span, axis=0)
                span *= 2
            if span < w:
                s = s[8:, :] + s[:-8, :]
                s = s[HALO - 8:, :]
            else:
                s = s[HALO:, :]
            if r0 == 0:
                pos = (lax.broadcasted_iota(jnp.int32, (HALO, gd), 0) + (j * tt + 1)).astype(jnp.float32)
                inv = jnp.concatenate([1.0 / jnp.minimum(pos, float(w)),
                                       jnp.full((sub - HALO, gd), 1.0 / w, jnp.float32)], axis=0)
            else:
                inv = 1.0 / w
            pooled.append(s * inv - u_p[:, cols])
        mixed = []
        for g in range(0, len(POOL_WINDOWS), 2):
            pair = jnp.concatenate(pooled[g:g + 2], axis=-1).astype(jnp.bfloat16)
            mixed.append(jnp.dot(pair, wpair_ref[g // 2], preferred_element_type=jnp.float32))
        return jnp.concatenate(mixed, axis=-1)

    def finish(r0, mixed, gate_p):
        rows = slice(r0, r0 + sub)
        y_ref[rows, cw:cw + pw] = (mixed * pscale_ref[...] * _silu(gate_p)).astype(jnp.bfloat16)
        yo = jnp.dot(y_ref[rows, :], wout_ref[...], preferred_element_type=jnp.float32)
        r2 = lax.rsqrt(jnp.mean(yo * yo, axis=-1, keepdims=True) + NORM_EPS)
        o_ref[rows, :] = x_ref[rows, :] + (yo * r2) * out_gain

    n_conv = po // cw
    n_proj = n_conv + 2 * pw // cw
    norm(0)
    conv_projs = project(0, 0, n_conv)
    pool_projs = project(0, n_conv, n_proj)
    if sub < tt:
        norm(sub)
    for r0 in range(0, tt, sub):
        more = r0 + sub < tt
        ahead = norm(r0 + 2 * sub) if r0 + 2 * sub < tt else None
        next_conv = project(r0 + sub, 0, n_conv) if more else None
        mix_conv(r0, conv_projs, ahead)
        mixed = mix_pool(r0, pool_projs[0])
        next_pool = project(r0 + sub, n_conv, n_proj) if more else None
        finish(r0, mixed, pool_projs[1])
        conv_projs, pool_projs = next_conv, next_pool


def _layer(l, x, mod, g_pre, g_post, w_in, w_conv, w_pool, pool_scale, w_out):
    batch, seq, d = x.shape
    cw = w_conv.shape[2]
    _, groups, gd, _ = w_pool.shape
    pw = groups * gd
    tt = SEQ_TILE
    per_layer = lambda b, j: (l, 0, 0)
    assert w_in.shape[1] == w_out.shape[1] and w_in.shape[1] % WEIGHT_CAST_ROWS == 0
    once = pl.Buffered(1)
    return pl.pallas_call(
        functools.partial(_layer_kernel, cw=cw, gd=gd, sub=SUB_ROWS),
        grid=(batch, seq // tt),
        in_specs=[
            pl.BlockSpec((None, tt, d), lambda b, j: (b, j, 0)),
            pl.BlockSpec((None, None, 3, d), lambda b, j: (l, b, 0, 0)),
            pl.BlockSpec((None, 1, d), per_layer),
            pl.BlockSpec((None, 1, d), per_layer),
            pl.BlockSpec((None,) + w_in.shape[1:], per_layer, pipeline_mode=once),
            pl.BlockSpec((None,) + w_conv.shape[1:], per_layer),
            pl.BlockSpec((None,) + w_pool.shape[1:], lambda b, j: (l, 0, 0, 0)),
            pl.BlockSpec((None, 1, pw), per_layer),
            pl.BlockSpec((None,) + w_out.shape[1:], per_layer, pipeline_mode=once),
        ],
        out_specs=pl.BlockSpec((None, tt, d), lambda b, j: (b, j, 0)),
        out_shape=jax.ShapeDtypeStruct(x.shape, x.dtype),
        scratch_shapes=[
            pltpu.VMEM(w_in.shape[1:], jnp.bfloat16),
            pltpu.VMEM(w_out.shape[1:], jnp.bfloat16),
            pltpu.VMEM((HALO, cw), jnp.float32),
            pltpu.VMEM((HALO, pw), jnp.float32),
            pltpu.VMEM((groups // 2, 2 * gd, 2 * gd), jnp.bfloat16),
            pltpu.VMEM((tt, d), jnp.bfloat16),
            pltpu.VMEM((tt, cw + pw), jnp.bfloat16),
        ],
        compiler_params=pltpu.CompilerParams(
            dimension_semantics=("arbitrary", "arbitrary"),
            vmem_limit_bytes=VMEM_LIMIT_BYTES),
        name="mixer_layer",
    )(x, mod, g_pre, g_post, w_in, w_conv, w_pool, pool_scale, w_out)


def kernel(x, c, w_ada, b_ada, g_pre, w_in, w_conv, w_pool, pool_scale, w_out, g_post):
    depth = w_ada.shape[0]
    batch, _, d = x.shape
    mod = _ada_modulation(c, w_ada, b_ada).reshape(depth, batch, 3, d)
    g_pre3 = g_pre.reshape(depth, 1, d)
    g_post3 = g_post.reshape(depth, 1, d)
    pool_scale3 = pool_scale.reshape(depth, 1, -1)
    for l in range(depth):
        x = _layer(l, x, mod, g_pre3, g_post3, w_in, w_conv, w_pool, pool_scale3, w_out)
    return x
```

```python
import functools

import jax
import jax.numpy as jnp
from jax import lax
from jax.experimental import pallas as pl
from jax.experimental.pallas import tpu as pltpu

CONV_K = 3
POOL_WINDOWS = (2, 4, 8, 16)
NORM_EPS = 1e-6
HALO = 16
SEQ_TILE = 1024
SUB_ROWS = 128
ADA_COL_TILE = 1024
WEIGHT_CAST_ROWS = 128
VMEM_LIMIT_BYTES = 56 * 1024 * 1024


def _silu(v):
    return v * jax.nn.sigmoid(v)


def _ada_kernel(c_ref, w_ref, b_ref, o_ref):
    c_act = _silu(c_ref[...]).astype(jnp.bfloat16)
    w = w_ref[...].astype(jnp.bfloat16)
    o_ref[...] = jnp.dot(c_act, w, preferred_element_type=jnp.float32) + b_ref[...]


def _ada_modulation(c, w_ada, b_ada):
    depth, d, cols = w_ada.shape
    batch = c.shape[0]
    tn = ADA_COL_TILE
    return pl.pallas_call(
        _ada_kernel,
        grid=(depth, cols // tn),
        in_specs=[
            pl.BlockSpec((batch, d), lambda l, n: (0, 0)),
            pl.BlockSpec((None, d, tn), lambda l, n: (l, 0, n)),
            pl.BlockSpec((None, 1, tn), lambda l, n: (l, 0, n)),
        ],
        out_specs=pl.BlockSpec((None, batch, tn), lambda l, n: (l, 0, n)),
        out_shape=jax.ShapeDtypeStruct((depth, batch, cols), jnp.float32),
        compiler_params=pltpu.CompilerParams(
            dimension_semantics=("arbitrary", "arbitrary"),
            vmem_limit_bytes=VMEM_LIMIT_BYTES),
        name="ada_modulation",
    )(c, w_ada, b_ada.reshape(depth, 1, cols))


def _layer_kernel(x_ref, mod_ref, gpre_ref, gpost_ref, win32_ref, wconv_ref, wpool_ref,
                  pscale_ref, wout32_ref, o_ref, win_ref, wout_ref, zc_ref, pc_ref, wpair_ref,
                  h_ref, y_ref, *, cw, gd, sub):
    j = pl.program_id(1)
    tt = x_ref.shape[0]
    pw = gd * len(POOL_WINDOWS)
    po = 4 * cw

    @pl.when(jnp.logical_and(pl.program_id(0) == 0, j == 0))
    def _():
        def cast_rows(i, carry):
            rows = pl.ds(pl.multiple_of(i * WEIGHT_CAST_ROWS, WEIGHT_CAST_ROWS), WEIGHT_CAST_ROWS)
            win_ref[rows, :] = win32_ref[rows, :].astype(jnp.bfloat16)
            wout_ref[rows, :] = wout32_ref[rows, :].astype(jnp.bfloat16)
            return carry
        lax.fori_loop(0, win_ref.shape[0] // WEIGHT_CAST_ROWS, cast_rows, 0)
        zero = jnp.zeros((gd, gd), jnp.bfloat16)
        for g in range(0, len(POOL_WINDOWS), 2):
            wpair_ref[g // 2] = jnp.concatenate(
                [jnp.concatenate([wpool_ref[g].astype(jnp.bfloat16), zero], axis=-1),
                 jnp.concatenate([zero, wpool_ref[g + 1].astype(jnp.bfloat16)], axis=-1)], axis=0)

    @pl.when(j == 0)
    def _():
        zc_ref[...] = jnp.zeros((HALO, cw), jnp.float32)
        pc_ref[...] = jnp.zeros((HALO, pw), jnp.float32)

    shift = mod_ref[0:1, :]
    gain = gpre_ref[...] * (1.0 + mod_ref[1:2, :])
    out_gain = mod_ref[2:3, :] * gpost_ref[...]

    def norm(r0):
        x = x_ref[r0:r0 + sub, :]
        r = lax.rsqrt(jnp.mean(x * x, axis=-1, keepdims=True) + NORM_EPS)
        h_ref[r0:r0 + sub, :] = ((x * r) * gain + shift).astype(jnp.bfloat16)

    def project(r0, first, last):
        return [jnp.dot(h_ref[r0:r0 + sub, :], win_ref[:, k * cw:(k + 1) * cw],
                        preferred_element_type=jnp.float32) for k in range(first, last)]

    def lagged(e, lag):
        return pltpu.roll(e, lag, axis=0)[HALO:, :]

    def mix_conv(r0, projs):
        u_a, b_a, c_a, gate_a = projs
        z = c_a * u_a
        ez = jnp.concatenate([zc_ref[...], z], axis=0)
        zc_ref[...] = z[sub - HALO:, :]
        conv = wconv_ref[CONV_K - 1:CONV_K, :] * z
        for k in range(CONV_K - 1):
            conv = conv + wconv_ref[k:k + 1, :] * lagged(ez, CONV_K - 1 - k)
        y_ref[r0:r0 + sub, 0:cw] = (b_a * conv * _silu(gate_a)).astype(jnp.bfloat16)

    def mix_pool(r0, u_p):
        ep = jnp.concatenate([pc_ref[...], u_p], axis=0)
        pc_ref[...] = u_p[sub - HALO:, :]
        pooled = []
        for g, w in enumerate(POOL_WINDOWS):
            cols = slice(g * gd, (g + 1) * gd)
            s = ep[:, cols]
            span = 1
            while span < min(w, 8):
                s = s + pltpu.roll(s, span, axis=0)
                span *= 2
            if span < w:
                s = s[8:, :] + s[:-8, :]
                s = s[HALO - 8:, :]
            else:
                s = s[HALO:, :]
            if r0 == 0:
                pos = (lax.broadcasted_iota(jnp.int32, (HALO, gd), 0) + (j * tt + 1)).astype(jnp.float32)
                inv = jnp.concatenate([1.0 / jnp.minimum(pos, float(w)),
                                       jnp.full((sub - HALO, gd), 1.0 / w, jnp.float32)], axis=0)
            else:
                inv = 1.0 / w
            pooled.append(s * inv - u_p[:, cols])
        mixed = []
        for g in range(0, len(POOL_WINDOWS), 2):
            pair = jnp.concatenate(pooled[g:g + 2], axis=-1).astype(jnp.bfloat16)
            mixed.append(jnp.dot(pair, wpair_ref[g // 2], preferred_element_type=jnp.float32))
        return jnp.concatenate(mixed, axis=-1)

    def finish(r0, mixed, gate_p):
        rows = slice(r0, r0 + sub)
        y_ref[rows, cw:cw + pw] = (mixed * pscale_ref[...] * _silu(gate_p)).astype(jnp.bfloat16)
        yo = jnp.dot(y_ref[rows, :], wout_ref[...], preferred_element_type=jnp.float32)
        r2 = lax.rsqrt(jnp.mean(yo * yo, axis=-1, keepdims=True) + NORM_EPS)
        o_ref[rows, :] = x_ref[rows, :] + (yo * r2) * out_gain

    n_conv = po // cw
    n_proj = n_conv + 2 * pw // cw
    norm(0)
    conv_projs = project(0, 0, n_conv)
    pool_projs = project(0, n_conv, n_proj)
    if sub < tt:
        norm(sub)
    for r0 in range(0, tt, sub):
        more = r0 + sub < tt
        if r0 + 2 * sub < tt:
            norm(r0 + 2 * sub)
        next_conv = project(r0 + sub, 0, n_conv) if more else None
        mix_conv(r0, conv_projs)
        mixed = mix_pool(r0, pool_projs[0])
        next_pool = project(r0 + sub, n_conv, n_proj) if more else None
        finish(r0, mixed, pool_projs[1])
        conv_projs, pool_projs = next_conv, next_pool


def _layer(l, x, mod, g_pre, g_post, w_in, w_conv, w_pool, pool_scale, w_out):
    batch, seq, d = x.shape
    cw = w_conv.shape[2]
    _, groups, gd, _ = w_pool.shape
    pw = groups * gd
    tt = SEQ_TILE
    per_layer = lambda b, j: (l, 0, 0)
    assert w_in.shape[1] == w_out.shape[1] and w_in.shape[1] % WEIGHT_CAST_ROWS == 0
    once = pl.Buffered(1)
    return pl.pallas_call(
        functools.partial(_layer_kernel, cw=cw, gd=gd, sub=SUB_ROWS),
        grid=(batch, seq // tt),
        in_specs=[
            pl.BlockSpec((None, tt, d), lambda b, j: (b, j, 0)),
            pl.BlockSpec((None, None, 3, d), lambda b, j: (l, b, 0, 0)),
            pl.BlockSpec((None, 1, d), per_layer),
            pl.BlockSpec((None, 1, d), per_layer),
            pl.BlockSpec((None,) + w_in.shape[1:], per_layer, pipeline_mode=once),
            pl.BlockSpec((None,) + w_conv.shape[1:], per_layer),
            pl.BlockSpec((None,) + w_pool.shape[1:], lambda b, j: (l, 0, 0, 0)),
            pl.BlockSpec((None, 1, pw), per_layer),
            pl.BlockSpec((None,) + w_out.shape[1:], per_layer, pipeline_mode=once),
        ],
        out_specs=pl.BlockSpec((None, tt, d), lambda b, j: (b, j, 0)),
        out_shape=jax.ShapeDtypeStruct(x.shape, x.dtype),
        scratch_shapes=[
            pltpu.VMEM(w_in.shape[1:], jnp.bfloat16),
            pltpu.VMEM(w_out.shape[1:], jnp.bfloat16),
            pltpu.VMEM((HALO, cw), jnp.float32),
            pltpu.VMEM((HALO, pw), jnp.float32),
            pltpu.VMEM((groups // 2, 2 * gd, 2 * gd), jnp.bfloat16),
            pltpu.VMEM((tt, d), jnp.bfloat16),
            pltpu.VMEM((tt, cw + pw), jnp.bfloat16),
        ],
        compiler_params=pltpu.CompilerParams(
            dimension_semantics=("arbitrary", "arbitrary"),
            vmem_limit_bytes=VMEM_LIMIT_BYTES),
        name="mixer_layer",
    )(x, mod, g_pre, g_post, w_in, w_conv, w_pool, pool_scale, w_out)


def kernel(x, c, w_ada, b_ada, g_pre, w_in, w_conv, w_pool, pool_scale, w_out, g_post):
    depth = w_ada.shape[0]
    batch, _, d = x.shape
    mod = _ada_modulation(c, w_ada, b_ada).reshape(depth, batch, 3, d)
    g_pre3 = g_pre.reshape(depth, 1, d)
    g_post3 = g_post.reshape(depth, 1, d)
    pool_scale3 = pool_scale.reshape(depth, 1, -1)
    for l in range(depth):
        x = _layer(l, x, mod, g_pre3, g_post3, w_in, w_conv, w_pool, pool_scale3, w_out)
    return x
```

```python
import functools

import jax
import jax.numpy as jnp
from jax import lax
from jax.experimental import pallas as pl
from jax.experimental.pallas import tpu as pltpu

CONV_K = 3
POOL_WINDOWS = (2, 4, 8, 16)
NORM_EPS = 1e-6
HALO = 16
SEQ_TILE = 1024
SUB_ROWS = 256
ADA_COL_TILE = 1024
WEIGHT_CAST_ROWS = 128
VMEM_LIMIT_BYTES = 56 * 1024 * 1024


def _silu(v):
    return v * jax.nn.sigmoid(v)


def _ada_kernel(c_ref, w_ref, b_ref, o_ref):
    c_act = _silu(c_ref[...]).astype(jnp.bfloat16)
    w = w_ref[...].astype(jnp.bfloat16)
    o_ref[...] = jnp.dot(c_act, w, preferred_element_type=jnp.float32) + b_ref[...]


def _ada_modulation(c, w_ada, b_ada):
    depth, d, cols = w_ada.shape
    batch = c.shape[0]
    tn = ADA_COL_TILE
    return pl.pallas_call(
        _ada_kernel,
        grid=(depth, cols // tn),
        in_specs=[
            pl.BlockSpec((batch, d), lambda l, n: (0, 0)),
            pl.BlockSpec((None, d, tn), lambda l, n: (l, 0, n)),
            pl.BlockSpec((None, 1, tn), lambda l, n: (l, 0, n)),
        ],
        out_specs=pl.BlockSpec((None, batch, tn), lambda l, n: (l, 0, n)),
        out_shape=jax.ShapeDtypeStruct((depth, batch, cols), jnp.float32),
        compiler_params=pltpu.CompilerParams(
            dimension_semantics=("arbitrary", "arbitrary"),
            vmem_limit_bytes=VMEM_LIMIT_BYTES),
        name="ada_modulation",
    )(c, w_ada, b_ada.reshape(depth, 1, cols))


def _proj_block_order(cw, pw, gd):
    nb = cw // gd
    order = []
    for d in range(nb):
        order += [0 * nb + d, 2 * nb + d, 1 * nb + d, 3 * nb + d]
    for g in range(0, pw // gd, 2):
        order += [4 * nb + g, 4 * nb + g + 1, 4 * nb + pw // gd + g, 4 * nb + pw // gd + g + 1]
    return order


def _layer_kernel(x_ref, mod_ref, gpre_ref, gpost_ref, win32_ref, wconv_ref, wpool_ref,
                  pscale_ref, wout32_ref, o_ref, win_ref, wout_ref, zc_ref, pc_ref, wpair_ref,
                  h_ref, y_ref, *, cw, gd, sub):
    j = pl.program_id(1)
    tt = x_ref.shape[0]
    n_pool = len(POOL_WINDOWS)
    pw = gd * n_pool
    n_conv = cw // gd
    n_pair = n_pool // 2
    blk = 4 * gd
    order = _proj_block_order(cw, pw, gd)

    @pl.when(jnp.logical_and(pl.program_id(0) == 0, j == 0))
    def _():
        def cast_rows(i, carry):
            rows = pl.ds(pl.multiple_of(i * WEIGHT_CAST_ROWS, WEIGHT_CAST_ROWS), WEIGHT_CAST_ROWS)
            for q, p in enumerate(order):
                win_ref[rows, q * gd:(q + 1) * gd] = win32_ref[rows, p * gd:(p + 1) * gd].astype(jnp.bfloat16)
            wout_ref[rows, :] = wout32_ref[rows, :].astype(jnp.bfloat16)
            return carry
        lax.fori_loop(0, win_ref.shape[0] // WEIGHT_CAST_ROWS, cast_rows, 0)
        zero = jnp.zeros((gd, gd), jnp.bfloat16)
        for g in range(0, n_pool, 2):
            wpair_ref[g // 2] = jnp.concatenate(
                [jnp.concatenate([wpool_ref[g].astype(jnp.bfloat16), zero], axis=-1),
                 jnp.concatenate([zero, wpool_ref[g + 1].astype(jnp.bfloat16)], axis=-1)], axis=0)

    @pl.when(j == 0)
    def _():
        zc_ref[...] = jnp.zeros((HALO, cw), jnp.float32)
        pc_ref[...] = jnp.zeros((HALO, pw), jnp.float32)

    shift = mod_ref[0:1, :]
    gain = gpre_ref[...] * (1.0 + mod_ref[1:2, :])
    out_gain = mod_ref[2:3, :] * gpost_ref[...]

    def norm(r0):
        x = x_ref[r0:r0 + sub, :]
        r = lax.rsqrt(jnp.mean(x * x, axis=-1, keepdims=True) + NORM_EPS)
        h_ref[r0:r0 + sub, :] = ((x * r) * gain + shift).astype(jnp.bfloat16)

    def project(r0, k):
        return jnp.dot(h_ref[r0:r0 + sub, :], win_ref[:, k * blk:(k + 1) * blk],
                       preferred_element_type=jnp.float32)

    def lagged(e, lag):
        return pltpu.roll(e, lag, axis=0)[HALO:, :]

    def mix_conv(r0, d, res):
        cols = slice(d * gd, (d + 1) * gd)
        u, c, b, gate = (res[:, k * gd:(k + 1) * gd] for k in range(4))
        z = c * u
        ez = jnp.concatenate([zc_ref[:, cols], z], axis=0)
        zc_ref[:, cols] = z[sub - HALO:, :]
        conv = wconv_ref[CONV_K - 1:CONV_K, cols] * z
        for k in range(CONV_K - 1):
            conv = conv + wconv_ref[k:k + 1, cols] * lagged(ez, CONV_K - 1 - k)
        y_ref[r0:r0 + sub, cols] = (b * conv * _silu(gate)).astype(jnp.bfloat16)

    def mix_pool(r0, pair, res):
        pooled = []
        for k in range(2):
            g = 2 * pair + k
            w = POOL_WINDOWS[g]
            cols = slice(g * gd, (g + 1) * gd)
            u = res[:, k * gd:(k + 1) * gd]
            s = jnp.concatenate([pc_ref[:, cols], u], axis=0)
            pc_ref[:, cols] = u[sub - HALO:, :]
            span = 1
            while span < min(w, 8):
                s = s + pltpu.roll(s, span, axis=0)
                span *= 2
            if span < w:
                s = s[8:, :] + s[:-8, :]
                s = s[HALO - 8:, :]
            else:
                s = s[HALO:, :]
            if r0 == 0:
                pos = (lax.broadcasted_iota(jnp.int32, (HALO, gd), 0) + (j * tt + 1)).astype(jnp.float32)
                inv = jnp.concatenate([1.0 / jnp.minimum(pos, float(w)),
                                       jnp.full((sub - HALO, gd), 1.0 / w, jnp.float32)], axis=0)
            else:
                inv = 1.0 / w
            pooled.append(s * inv - u)
        mapped = jnp.dot(jnp.concatenate(pooled, axis=-1).astype(jnp.bfloat16), wpair_ref[pair],
                         preferred_element_type=jnp.float32)
        return mapped, res[:, 2 * gd:4 * gd]

    def gate_pool(r0, pair, mapped, gate):
        cols = slice(pair * 2 * gd, (pair + 1) * 2 * gd)
        y_ref[r0:r0 + sub, cw + pair * 2 * gd:cw + (pair + 1) * 2 * gd] = (
            mapped * pscale_ref[:, cols] * _silu(gate)).astype(jnp.bfloat16)

    def finish(r0):
        rows = slice(r0, r0 + sub)
        yo = jnp.dot(y_ref[rows, :], wout_ref[...], preferred_element_type=jnp.float32)
        r2 = lax.rsqrt(jnp.mean(yo * yo, axis=-1, keepdims=True) + NORM_EPS)
        o_ref[rows, :] = x_ref[rows, :] + (yo * r2) * out_gain

    assert n_conv >= 2 and n_pair == 2
    norm(0)
    head = [project(0, 0), project(0, 1)]
    for r0 in range(0, tt, sub):
        nxt = r0 + sub if r0 + sub < tt else None
        if nxt is not None:
            norm(nxt)
        res = head
        for d in range(n_conv):
            if d + 2 < n_conv + n_pair:
                res.append(project(r0, d + 2))
            mix_conv(r0, d, res[d])
        head = []
        mapped = []
        for p in range(n_pair):
            mapped.append(mix_pool(r0, p, res[n_conv + p]))
            if nxt is not None:
                head.append(project(nxt, p))
        for p in range(n_pair):
            gate_pool(r0, p, *mapped[p])
        finish(r0)


def _layer(l, x, mod, g_pre, g_post, w_in, w_conv, w_pool, pool_scale, w_out):
    batch, seq, d = x.shape
    cw = w_conv.shape[2]
    _, groups, gd, _ = w_pool.shape
    pw = groups * gd
    tt = SEQ_TILE
    per_layer = lambda b, j: (l, 0, 0)
    assert w_in.shape[1] == w_out.shape[1] and w_in.shape[1] % WEIGHT_CAST_ROWS == 0
    assert cw == pw and cw % gd == 0 and groups % 2 == 0
    once = pl.Buffered(1)
    return pl.pallas_call(
        functools.partial(_layer_kernel, cw=cw, gd=gd, sub=SUB_ROWS),
        grid=(batch, seq // tt),
        in_specs=[
            pl.BlockSpec((None, tt, d), lambda b, j: (b, j, 0)),
            pl.BlockSpec((None, None, 3, d), lambda b, j: (l, b, 0, 0)),
            pl.BlockSpec((None, 1, d), per_layer),
            pl.BlockSpec((None, 1, d), per_layer),
            pl.BlockSpec((None,) + w_in.shape[1:], per_layer, pipeline_mode=once),
            pl.BlockSpec((None,) + w_conv.shape[1:], per_layer),
            pl.BlockSpec((None,) + w_pool.shape[1:], lambda b, j: (l, 0, 0, 0)),
            pl.BlockSpec((None, 1, pw), per_layer),
            pl.BlockSpec((None,) + w_out.shape[1:], per_layer, pipeline_mode=once),
        ],
        out_specs=pl.BlockSpec((None, tt, d), lambda b, j: (b, j, 0)),
        out_shape=jax.ShapeDtypeStruct(x.shape, x.dtype),
        scratch_shapes=[
            pltpu.VMEM(w_in.shape[1:], jnp.bfloat16),
            pltpu.VMEM(w_out.shape[1:], jnp.bfloat16),
            pltpu.VMEM((HALO, cw), jnp.float32),
            pltpu.VMEM((HALO, pw), jnp.float32),
            pltpu.VMEM((groups // 2, 2 * gd, 2 * gd), jnp.bfloat16),
            pltpu.VMEM((tt, d), jnp.bfloat16),
            pltpu.VMEM((tt, cw + pw), jnp.bfloat16),
        ],
        compiler_params=pltpu.CompilerParams(
            dimension_semantics=("arbitrary", "arbitrary"),
            vmem_limit_bytes=VMEM_LIMIT_BYTES),
        name="mixer_layer",
    )(x, mod, g_pre, g_post, w_in, w_conv, w_pool, pool_scale, w_out)


def kernel(x, c, w_ada, b_ada, g_pre, w_in, w_conv, w_pool, pool_scale, w_out, g_post):
    depth = w_ada.shape[0]
    batch, _, d = x.shape
    mod = _ada_modulation(c, w_ada, b_ada).reshape(depth, batch, 3, d)
    g_pre3 = g_pre.reshape(depth, 1, d)
    g_post3 = g_post.reshape(depth, 1, d)
    pool_scale3 = pool_scale.reshape(depth, 1, -1)
    for l in range(depth):
        x = _layer(l, x, mod, g_pre3, g_post3, w_in, w_conv, w_pool, pool_scale3, w_out)
    return x
```

```python
import functools

import jax
import jax.numpy as jnp
from jax import lax
from jax.experimental import pallas as pl
from jax.experimental.pallas import tpu as pltpu

CONV_K = 3
POOL_WINDOWS = (2, 4, 8, 16)
NORM_EPS = 1e-6
HALO = 16
SEQ_TILE = 1024
SUB_ROWS = 256
ADA_COL_TILE = 1024
WEIGHT_CAST_ROWS = 128
VMEM_LIMIT_BYTES = 56 * 1024 * 1024


def _silu(v):
    return v * jax.nn.sigmoid(v)


def _ada_kernel(c_ref, w_ref, b_ref, o_ref):
    c_act = _silu(c_ref[...]).astype(jnp.bfloat16)
    w = w_ref[...].astype(jnp.bfloat16)
    o_ref[...] = jnp.dot(c_act, w, preferred_element_type=jnp.float32) + b_ref[...]


def _ada_modulation(c, w_ada, b_ada):
    depth, d, cols = w_ada.shape
    batch = c.shape[0]
    tn = ADA_COL_TILE
    return pl.pallas_call(
        _ada_kernel,
        grid=(depth, cols // tn),
        in_specs=[
            pl.BlockSpec((batch, d), lambda l, n: (0, 0)),
            pl.BlockSpec((None, d, tn), lambda l, n: (l, 0, n)),
            pl.BlockSpec((None, 1, tn), lambda l, n: (l, 0, n)),
        ],
        out_specs=pl.BlockSpec((None, batch, tn), lambda l, n: (l, 0, n)),
        out_shape=jax.ShapeDtypeStruct((depth, batch, cols), jnp.float32),
        compiler_params=pltpu.CompilerParams(
            dimension_semantics=("arbitrary", "arbitrary"),
            vmem_limit_bytes=VMEM_LIMIT_BYTES),
        name="ada_modulation",
    )(c, w_ada, b_ada.reshape(depth, 1, cols))


def _proj_block_order(cw, pw, gd):
    nb = cw // gd
    order = []
    for d in range(nb):
        order += [0 * nb + d, 2 * nb + d, 1 * nb + d, 3 * nb + d]
    for g in range(0, pw // gd, 2):
        order += [4 * nb + g, 4 * nb + g + 1, 4 * nb + pw // gd + g, 4 * nb + pw // gd + g + 1]
    return order


def _layer_kernel(x_ref, xn_ref, mod_ref, modn_ref, gpre_ref, gpost_ref, win32_ref, wconv_ref, wpool_ref,
                  pscale_ref, wout32_ref, o_ref, win_ref, wout_ref, zc_ref, pc_ref, wpair_ref,
                  h_ref, y_ref, head_ref, *, cw, gd, sub):
    j = pl.program_id(1)
    tt = x_ref.shape[0]
    n_pool = len(POOL_WINDOWS)
    pw = gd * n_pool
    n_conv = cw // gd
    n_pair = n_pool // 2
    blk = 4 * gd
    order = _proj_block_order(cw, pw, gd)

    @pl.when(jnp.logical_and(pl.program_id(0) == 0, j == 0))
    def _():
        def cast_rows(i, carry):
            rows = pl.ds(pl.multiple_of(i * WEIGHT_CAST_ROWS, WEIGHT_CAST_ROWS), WEIGHT_CAST_ROWS)
            for q, p in enumerate(order):
                win_ref[rows, q * gd:(q + 1) * gd] = win32_ref[rows, p * gd:(p + 1) * gd].astype(jnp.bfloat16)
            wout_ref[rows, :] = wout32_ref[rows, :].astype(jnp.bfloat16)
            return carry
        lax.fori_loop(0, win_ref.shape[0] // WEIGHT_CAST_ROWS, cast_rows, 0)
        zero = jnp.zeros((gd, gd), jnp.bfloat16)
        for g in range(0, n_pool, 2):
            wpair_ref[g // 2] = jnp.concatenate(
                [jnp.concatenate([wpool_ref[g].astype(jnp.bfloat16), zero], axis=-1),
                 jnp.concatenate([zero, wpool_ref[g + 1].astype(jnp.bfloat16)], axis=-1)], axis=0)

    def norm(src_ref, r0, m_ref):
        x = src_ref[r0:r0 + sub, :]
        r = lax.rsqrt(jnp.mean(x * x, axis=-1, keepdims=True) + NORM_EPS)
        gain = gpre_ref[...] * (1.0 + m_ref[1:2, :])
        h_ref[r0:r0 + sub, :] = ((x * r) * gain + m_ref[0:1, :]).astype(jnp.bfloat16)

    def project(r0, k):
        return jnp.dot(h_ref[r0:r0 + sub, :], win_ref[:, k * blk:(k + 1) * blk],
                       preferred_element_type=jnp.float32)

    @pl.when(jnp.logical_and(pl.program_id(0) == 0, j == 0))
    def _():
        norm(x_ref, 0, mod_ref)
        for p in range(2):
            head_ref[p] = project(0, p)

    @pl.when(j == 0)
    def _():
        zc_ref[...] = jnp.zeros((HALO, cw), jnp.float32)
        pc_ref[...] = jnp.zeros((HALO, pw), jnp.float32)

    out_gain = mod_ref[2:3, :] * gpost_ref[...]

    def lagged(e, lag):
        return pltpu.roll(e, lag, axis=0)[HALO:, :]

    def mix_conv(r0, d, res):
        cols = slice(d * gd, (d + 1) * gd)
        u, c, b, gate = (res[:, k * gd:(k + 1) * gd] for k in range(4))
        z = c * u
        ez = jnp.concatenate([zc_ref[:, cols], z], axis=0)
        zc_ref[:, cols] = z[sub - HALO:, :]
        conv = wconv_ref[CONV_K - 1:CONV_K, cols] * z
        for k in range(CONV_K - 1):
            conv = conv + wconv_ref[k:k + 1, cols] * lagged(ez, CONV_K - 1 - k)
        y_ref[r0:r0 + sub, cols] = (b * conv * _silu(gate)).astype(jnp.bfloat16)

    def mix_pool(r0, pair, res):
        pooled = []
        for k in range(2):
            g = 2 * pair + k
            w = POOL_WINDOWS[g]
            cols = slice(g * gd, (g + 1) * gd)
            u = res[:, k * gd:(k + 1) * gd]
            s = jnp.concatenate([pc_ref[:, cols], u], axis=0)
            pc_ref[:, cols] = u[sub - HALO:, :]
            span = 1
            while span < min(w, 8):
                s = s + pltpu.roll(s, span, axis=0)
                span *= 2
            if span < w:
                s = s[8:, :] + s[:-8, :]
                s = s[HALO - 8:, :]
            else:
                s = s[HALO:, :]
            if r0 == 0:
                pos = (lax.broadcasted_iota(jnp.int32, (HALO, gd), 0) + (j * tt + 1)).astype(jnp.float32)
                inv = jnp.concatenate([1.0 / jnp.minimum(pos, float(w)),
                                       jnp.full((sub - HALO, gd), 1.0 / w, jnp.float32)], axis=0)
            else:
                inv = 1.0 / w
            pooled.append(s * inv - u)
        mapped = jnp.dot(jnp.concatenate(pooled, axis=-1).astype(jnp.bfloat16), wpair_ref[pair],
                         preferred_element_type=jnp.float32)
        return mapped, res[:, 2 * gd:4 * gd]

    def gate_pool(r0, pair, mapped, gate):
        cols = slice(pair * 2 * gd, (pair + 1) * 2 * gd)
        y_ref[r0:r0 + sub, cw + pair * 2 * gd:cw + (pair + 1) * 2 * gd] = (
            mapped * pscale_ref[:, cols] * _silu(gate)).astype(jnp.bfloat16)

    def finish(r0):
        rows = slice(r0, r0 + sub)
        yo = jnp.dot(y_ref[rows, :], wout_ref[...], preferred_element_type=jnp.float32)
        r2 = lax.rsqrt(jnp.mean(yo * yo, axis=-1, keepdims=True) + NORM_EPS)
        o_ref[rows, :] = x_ref[rows, :] + (yo * r2) * out_gain

    assert n_conv >= 2 and n_pair == 2
    head = [head_ref[0], head_ref[1]]
    for r0 in range(0, tt, sub):
        nxt = r0 + sub if r0 + sub < tt else None
        if nxt is not None:
            norm(x_ref, nxt, mod_ref)
        else:
            norm(xn_ref, 0, modn_ref)
        res = head
        for d in range(n_conv):
            if d + 2 < n_conv + n_pair:
                res.append(project(r0, d + 2))
            mix_conv(r0, d, res[d])
        head = []
        mapped = []
        for p in range(n_pair):
            mapped.append(mix_pool(r0, p, res[n_conv + p]))
            if nxt is not None:
                head.append(project(nxt, p))
            else:
                head_ref[p] = project(0, p)
        for p in range(n_pair):
            gate_pool(r0, p, *mapped[p])
        finish(r0)


def _layer(l, x, mod, g_pre, g_post, w_in, w_conv, w_pool, pool_scale, w_out):
    batch, seq, d = x.shape
    cw = w_conv.shape[2]
    _, groups, gd, _ = w_pool.shape
    pw = groups * gd
    tt = SEQ_TILE
    per_layer = lambda b, j: (l, 0, 0)
    n_tiles = seq // tt

    def next_tile(b, j):
        f = jnp.minimum(b * n_tiles + j + 1, batch * n_tiles - 1)
        return f // n_tiles, f % n_tiles

    def next_chain(b, j):
        nb, nj = next_tile(b, j)
        return nb, nj * (tt // SUB_ROWS), 0

    assert w_in.shape[1] == w_out.shape[1] and w_in.shape[1] % WEIGHT_CAST_ROWS == 0
    assert cw == pw and cw % gd == 0 and groups % 2 == 0
    once = pl.Buffered(1)
    return pl.pallas_call(
        functools.partial(_layer_kernel, cw=cw, gd=gd, sub=SUB_ROWS),
        grid=(batch, seq // tt),
        in_specs=[
            pl.BlockSpec((None, tt, d), lambda b, j: (b, j, 0)),
            pl.BlockSpec((None, SUB_ROWS, d), next_chain),
            pl.BlockSpec((None, None, 3, d), lambda b, j: (l, b, 0, 0)),
            pl.BlockSpec((None, None, 3, d), lambda b, j: (l, next_tile(b, j)[0], 0, 0)),
            pl.BlockSpec((None, 1, d), per_layer),
            pl.BlockSpec((None, 1, d), per_layer),
            pl.BlockSpec((None,) + w_in.shape[1:], per_layer, pipeline_mode=once),
            pl.BlockSpec((None,) + w_conv.shape[1:], per_layer),
            pl.BlockSpec((None,) + w_pool.shape[1:], lambda b, j: (l, 0, 0, 0)),
            pl.BlockSpec((None, 1, pw), per_layer),
            pl.BlockSpec((None,) + w_out.shape[1:], per_layer, pipeline_mode=once),
        ],
        out_specs=pl.BlockSpec((None, tt, d), lambda b, j: (b, j, 0)),
        out_shape=jax.ShapeDtypeStruct(x.shape, x.dtype),
        scratch_shapes=[
            pltpu.VMEM(w_in.shape[1:], jnp.bfloat16),
            pltpu.VMEM(w_out.shape[1:], jnp.bfloat16),
            pltpu.VMEM((HALO, cw), jnp.float32),
            pltpu.VMEM((HALO, pw), jnp.float32),
            pltpu.VMEM((groups // 2, 2 * gd, 2 * gd), jnp.bfloat16),
            pltpu.VMEM((tt, d), jnp.bfloat16),
            pltpu.VMEM((tt, cw + pw), jnp.bfloat16),
            pltpu.VMEM((2, SUB_ROWS, 4 * gd), jnp.float32),
        ],
        compiler_params=pltpu.CompilerParams(
            dimension_semantics=("arbitrary", "arbitrary"),
            vmem_limit_bytes=VMEM_LIMIT_BYTES),
        name="mixer_layer",
    )(x, x, mod, mod, g_pre, g_post, w_in, w_conv, w_pool, pool_scale, w_out)


def kernel(x, c, w_ada, b_ada, g_pre, w_in, w_conv, w_pool, pool_scale, w_out, g_post):
    depth = w_ada.shape[0]
    batch, _, d = x.shape
    mod = _ada_modulation(c, w_ada, b_ada).reshape(depth, batch, 3, d)
    g_pre3 = g_pre.reshape(depth, 1, d)
    g_post3 = g_post.reshape(depth, 1, d)
    pool_scale3 = pool_scale.reshape(depth, 1, -1)
    for l in range(depth):
        x = _layer(l, x, mod, g_pre3, g_post3, w_in, w_conv, w_pool, pool_scale3, w_out)
    return x
```

```python
import functools

import jax
import jax.numpy as jnp
from jax import lax
from jax.experimental import pallas as pl
from jax.experimental.pallas import tpu as pltpu

CONV_K = 3
POOL_WINDOWS = (2, 4, 8, 16)
NORM_EPS = 1e-6
HALO = 16
SEQ_TILE = 512
SUB_ROWS = 256
ADA_COL_TILE = 1024
WEIGHT_CAST_ROWS = 128
VMEM_LIMIT_BYTES = 56 * 1024 * 1024


def _silu(v):
    return v * jax.nn.sigmoid(v)


def _ada_kernel(c_ref, w_ref, b_ref, o_ref):
    c_act = _silu(c_ref[...]).astype(jnp.bfloat16)
    w = w_ref[...].astype(jnp.bfloat16)
    o_ref[...] = jnp.dot(c_act, w, preferred_element_type=jnp.float32) + b_ref[...]


def _ada_modulation(c, w_ada, b_ada):
    depth, d, cols = w_ada.shape
    batch = c.shape[0]
    tn = ADA_COL_TILE
    return pl.pallas_call(
        _ada_kernel,
        grid=(depth, cols // tn),
        in_specs=[
            pl.BlockSpec((batch, d), lambda l, n: (0, 0)),
            pl.BlockSpec((None, d, tn), lambda l, n: (l, 0, n)),
            pl.BlockSpec((None, 1, tn), lambda l, n: (l, 0, n)),
        ],
        out_specs=pl.BlockSpec((None, batch, tn), lambda l, n: (l, 0, n)),
        out_shape=jax.ShapeDtypeStruct((depth, batch, cols), jnp.float32),
        compiler_params=pltpu.CompilerParams(
            dimension_semantics=("arbitrary", "arbitrary"),
            vmem_limit_bytes=VMEM_LIMIT_BYTES),
        name="ada_modulation",
    )(c, w_ada, b_ada.reshape(depth, 1, cols))


def _proj_block_order(cw, pw, gd):
    nb = cw // gd
    order = []
    for d in range(nb):
        order += [0 * nb + d, 2 * nb + d, 1 * nb + d, 3 * nb + d]
    for g in range(0, pw // gd, 2):
        order += [4 * nb + g, 4 * nb + g + 1, 4 * nb + pw // gd + g, 4 * nb + pw // gd + g + 1]
    return order


def _layer_kernel(x_ref, mod_ref, gpre_ref, gpost_ref, win32_ref, wconv_ref, wpool_ref,
                  pscale_ref, wout32_ref, o_ref, win_ref, wout_ref, zc_ref, pc_ref, wpair_ref,
                  h_ref, y_ref, *, cw, gd, sub):
    j = pl.program_id(1)
    tt = x_ref.shape[0]
    n_pool = len(POOL_WINDOWS)
    pw = gd * n_pool
    n_conv = cw // gd
    n_pair = n_pool // 2
    blk = 4 * gd
    order = _proj_block_order(cw, pw, gd)

    @pl.when(jnp.logical_and(pl.program_id(0) == 0, j == 0))
    def _():
        def cast_rows(i, carry):
            rows = pl.ds(pl.multiple_of(i * WEIGHT_CAST_ROWS, WEIGHT_CAST_ROWS), WEIGHT_CAST_ROWS)
            for q, p in enumerate(order):
                win_ref[rows, q * gd:(q + 1) * gd] = win32_ref[rows, p * gd:(p + 1) * gd].astype(jnp.bfloat16)
            wout_ref[rows, :] = wout32_ref[rows, :].astype(jnp.bfloat16)
            return carry
        lax.fori_loop(0, win_ref.shape[0] // WEIGHT_CAST_ROWS, cast_rows, 0)
        zero = jnp.zeros((gd, gd), jnp.bfloat16)
        for g in range(0, n_pool, 2):
            wpair_ref[g // 2] = jnp.concatenate(
                [jnp.concatenate([wpool_ref[g].astype(jnp.bfloat16), zero], axis=-1),
                 jnp.concatenate([zero, wpool_ref[g + 1].astype(jnp.bfloat16)], axis=-1)], axis=0)

    @pl.when(j == 0)
    def _():
        zc_ref[...] = jnp.zeros((HALO, cw), jnp.float32)
        pc_ref[...] = jnp.zeros((HALO, pw), jnp.float32)

    shift = mod_ref[0:1, :]
    gain = gpre_ref[...] * (1.0 + mod_ref[1:2, :])
    out_gain = mod_ref[2:3, :] * gpost_ref[...]

    def norm(r0):
        x = x_ref[r0:r0 + sub, :]
        r = lax.rsqrt(jnp.mean(x * x, axis=-1, keepdims=True) + NORM_EPS)
        h_ref[r0:r0 + sub, :] = ((x * r) * gain + shift).astype(jnp.bfloat16)

    def project(r0, k):
        return jnp.dot(h_ref[r0:r0 + sub, :], win_ref[:, k * blk:(k + 1) * blk],
                       preferred_element_type=jnp.float32)

    def lagged(e, lag):
        return pltpu.roll(e, lag, axis=0)[HALO:, :]

    def mix_conv(r0, d, res):
        cols = slice(d * gd, (d + 1) * gd)
        u, c, b, gate = (res[:, k * gd:(k + 1) * gd] for k in range(4))
        z = c * u
        ez = jnp.concatenate([zc_ref[:, cols], z], axis=0)
        zc_ref[:, cols] = z[sub - HALO:, :]
        conv = wconv_ref[CONV_K - 1:CONV_K, cols] * z
        for k in range(CONV_K - 1):
            conv = conv + wconv_ref[k:k + 1, cols] * lagged(ez, CONV_K - 1 - k)
        y_ref[r0:r0 + sub, cols] = (b * conv * _silu(gate)).astype(jnp.bfloat16)

    def mix_pool(r0, pair, res):
        pooled = []
        for k in range(2):
            g = 2 * pair + k
            w = POOL_WINDOWS[g]
            cols = slice(g * gd, (g + 1) * gd)
            u = res[:, k * gd:(k + 1) * gd]
            s = jnp.concatenate([pc_ref[:, cols], u], axis=0)
            pc_ref[:, cols] = u[sub - HALO:, :]
            span = 1
            while span < min(w, 8):
                s = s + pltpu.roll(s, span, axis=0)
                span *= 2
            if span < w:
                s = s[8:, :] + s[:-8, :]
                s = s[HALO - 8:, :]
            else:
                s = s[HALO:, :]
            if r0 == 0:
                pos = (lax.broadcasted_iota(jnp.int32, (HALO, gd), 0) + (j * tt + 1)).astype(jnp.float32)
                inv = jnp.concatenate([1.0 / jnp.minimum(pos, float(w)),
                                       jnp.full((sub - HALO, gd), 1.0 / w, jnp.float32)], axis=0)
            else:
                inv = 1.0 / w
            pooled.append(s * inv - u)
        mapped = jnp.dot(jnp.concatenate(pooled, axis=-1).astype(jnp.bfloat16), wpair_ref[pair],
                         preferred_element_type=jnp.float32)
        return mapped, res[:, 2 * gd:4 * gd]

    def gate_pool(r0, pair, mapped, gate):
        cols = slice(pair * 2 * gd, (pair + 1) * 2 * gd)
        y_ref[r0:r0 + sub, cw + pair * 2 * gd:cw + (pair + 1) * 2 * gd] = (
            mapped * pscale_ref[:, cols] * _silu(gate)).astype(jnp.bfloat16)

    def finish(r0):
        rows = slice(r0, r0 + sub)
        yo = jnp.dot(y_ref[rows, :], wout_ref[...], preferred_element_type=jnp.float32)
        r2 = lax.rsqrt(jnp.mean(yo * yo, axis=-1, keepdims=True) + NORM_EPS)
        o_ref[rows, :] = x_ref[rows, :] + (yo * r2) * out_gain

    assert n_conv >= 2 and n_pair == 2
    norm(0)
    head = [project(0, 0), project(0, 1)]
    for r0 in range(0, tt, sub):
        nxt = r0 + sub if r0 + sub < tt else None
        if nxt is not None:
            norm(nxt)
        res = head
        for d in range(n_conv):
            if d + 2 < n_conv + n_pair:
                res.append(project(r0, d + 2))
            mix_conv(r0, d, res[d])
        head = []
        mapped = []
        for p in range(n_pair):
            mapped.append(mix_pool(r0, p, res[n_conv + p]))
            if nxt is not None:
                head.append(project(nxt, p))
        for p in range(n_pair):
            gate_pool(r0, p, *mapped[p])
        finish(r0)


def _layer(l, x, mod, g_pre, g_post, w_in, w_conv, w_pool, pool_scale, w_out):
    batch, seq, d = x.shape
    cw = w_conv.shape[2]
    _, groups, gd, _ = w_pool.shape
    pw = groups * gd
    tt = SEQ_TILE
    per_layer = lambda b, j: (l, 0, 0)
    assert w_in.shape[1] == w_out.shape[1] and w_in.shape[1] % WEIGHT_CAST_ROWS == 0
    assert cw == pw and cw % gd == 0 and groups % 2 == 0
    once = pl.Buffered(1)
    return pl.pallas_call(
        functools.partial(_layer_kernel, cw=cw, gd=gd, sub=SUB_ROWS),
        grid=(batch, seq // tt),
        in_specs=[
            pl.BlockSpec((None, tt, d), lambda b, j: (b, j, 0)),
            pl.BlockSpec((None, None, 3, d), lambda b, j: (l, b, 0, 0)),
            pl.BlockSpec((None, 1, d), per_layer),
            pl.BlockSpec((None, 1, d), per_layer),
            pl.BlockSpec((None,) + w_in.shape[1:], per_layer, pipeline_mode=once),
            pl.BlockSpec((None,) + w_conv.shape[1:], per_layer),
            pl.BlockSpec((None,) + w_pool.shape[1:], lambda b, j: (l, 0, 0, 0)),
            pl.BlockSpec((None, 1, pw), per_layer),
            pl.BlockSpec((None,) + w_out.shape[1:], per_layer, pipeline_mode=once),
        ],
        out_specs=pl.BlockSpec((None, tt, d), lambda b, j: (b, j, 0)),
        out_shape=jax.ShapeDtypeStruct(x.shape, x.dtype),
        scratch_shapes=[
            pltpu.VMEM(w_in.shape[1:], jnp.bfloat16),
            pltpu.VMEM(w_out.shape[1:], jnp.bfloat16),
            pltpu.VMEM((HALO, cw), jnp.float32),
            pltpu.VMEM((HALO, pw), jnp.float32),
            pltpu.VMEM((groups // 2, 2 * gd, 2 * gd), jnp.bfloat16),
            pltpu.VMEM((tt, d), jnp.bfloat16),
            pltpu.VMEM((tt, cw + pw), jnp.bfloat16),
        ],
        compiler_params=pltpu.CompilerParams(
            dimension_semantics=("arbitrary", "arbitrary"),
            vmem_limit_bytes=VMEM_LIMIT_BYTES),
        name="mixer_layer",
    )(x, mod, g_pre, g_post, w_in, w_conv, w_pool, pool_scale, w_out)


def kernel(x, c, w_ada, b_ada, g_pre, w_in, w_conv, w_pool, pool_scale, w_out, g_post):
    depth = w_ada.shape[0]
    batch, _, d = x.shape
    mod = _ada_modulation(c, w_ada, b_ada).reshape(depth, batch, 3, d)
    g_pre3 = g_pre.reshape(depth, 1, d)
    g_post3 = g_post.reshape(depth, 1, d)
    pool_scale3 = pool_scale.reshape(depth, 1, -1)
    for l in range(depth):
        x = _layer(l, x, mod, g_pre3, g_post3, w_in, w_conv, w_pool, pool_scale3, w_out)
    return x
```

```python
import functools

import jax
import jax.numpy as jnp
from jax import lax
from jax.experimental import pallas as pl
from jax.experimental.pallas import tpu as pltpu

CONV_K = 3
POOL_WINDOWS = (2, 4, 8, 16)
NORM_EPS = 1e-6
HALO = 16
SEQ_TILE = 1024
SUB_ROWS = 256
ADA_COL_TILE = 1024
WEIGHT_CAST_ROWS = 128
VMEM_LIMIT_BYTES = 56 * 1024 * 1024


def _silu(v):
    return v * jax.nn.sigmoid(v)


def _ada_kernel(c_ref, w_ref, b_ref, o_ref):
    c_act = _silu(c_ref[...]).astype(jnp.bfloat16)
    w = w_ref[...].astype(jnp.bfloat16)
    o_ref[...] = jnp.dot(c_act, w, preferred_element_type=jnp.float32) + b_ref[...]


def _ada_modulation(c, w_ada, b_ada):
    depth, d, cols = w_ada.shape
    batch = c.shape[0]
    tn = ADA_COL_TILE
    return pl.pallas_call(
        _ada_kernel,
        grid=(depth, cols // tn),
        in_specs=[
            pl.BlockSpec((batch, d), lambda l, n: (0, 0)),
            pl.BlockSpec((None, d, tn), lambda l, n: (l, 0, n)),
            pl.BlockSpec((None, 1, tn), lambda l, n: (l, 0, n)),
        ],
        out_specs=pl.BlockSpec((None, batch, tn), lambda l, n: (l, 0, n)),
        out_shape=jax.ShapeDtypeStruct((depth, batch, cols), jnp.float32),
        compiler_params=pltpu.CompilerParams(
            dimension_semantics=("arbitrary", "arbitrary"),
            vmem_limit_bytes=VMEM_LIMIT_BYTES),
        name="ada_modulation",
    )(c, w_ada, b_ada.reshape(depth, 1, cols))


def _proj_block_order(cw, pw, gd):
    nb = cw // gd
    order = []
    for d in range(nb):
        order += [0 * nb + d, 2 * nb + d, 1 * nb + d, 3 * nb + d]
    for g in range(0, pw // gd, 2):
        order += [4 * nb + g, 4 * nb + g + 1, 4 * nb + pw // gd + g, 4 * nb + pw // gd + g + 1]
    return order


def _layer_kernel(x_ref, mod_ref, gpre_ref, gpost_ref, win32_ref, wconv_ref, wpool_ref,
                  pscale_ref, wout32_ref, o_ref, win_ref, wout_ref, zc_ref, pc_ref, wpair_ref,
                  h_ref, y_ref, *, cw, gd, sub):
    j = pl.program_id(1)
    tt = x_ref.shape[0]
    n_pool = len(POOL_WINDOWS)
    pw = gd * n_pool
    n_conv = cw // gd
    n_pair = n_pool // 2
    blk = 4 * gd
    order = _proj_block_order(cw, pw, gd)

    @pl.when(jnp.logical_and(pl.program_id(0) == 0, j == 0))
    def _():
        zero = jnp.zeros((gd, gd), jnp.float32)
        for g in range(0, n_pool, 2):
            scaled = [wpool_ref[g + k] * pscale_ref[:, (g + k) * gd:(g + k + 1) * gd] for k in range(2)]
            wpair_ref[g // 2] = jnp.concatenate(
                [jnp.concatenate([scaled[0], zero], axis=-1),
                 jnp.concatenate([zero, scaled[1]], axis=-1)], axis=0).astype(jnp.bfloat16)

        def cast_rows(i, carry):
            rows = pl.ds(pl.multiple_of(i * WEIGHT_CAST_ROWS, WEIGHT_CAST_ROWS), WEIGHT_CAST_ROWS)
            for q, p in enumerate(order):
                win_ref[rows, q * gd:(q + 1) * gd] = win32_ref[rows, p * gd:(p + 1) * gd].astype(jnp.bfloat16)
            for pair in range(n_pair):
                dst = slice((n_conv + pair) * blk, (n_conv + pair) * blk + 2 * gd)
                win_ref[rows, dst] = jnp.dot(win_ref[rows, dst], wpair_ref[pair],
                                             preferred_element_type=jnp.float32).astype(jnp.bfloat16)
            wout_ref[rows, :] = wout32_ref[rows, :].astype(jnp.bfloat16)
            return carry
        lax.fori_loop(0, win_ref.shape[0] // WEIGHT_CAST_ROWS, cast_rows, 0)

    @pl.when(j == 0)
    def _():
        zc_ref[...] = jnp.zeros((HALO, cw), jnp.float32)
        pc_ref[...] = jnp.zeros((HALO, pw), jnp.float32)

    shift = mod_ref[0:1, :]
    gain = gpre_ref[...] * (1.0 + mod_ref[1:2, :])
    out_gain = mod_ref[2:3, :] * gpost_ref[...]

    def norm(r0):
        x = x_ref[r0:r0 + sub, :]
        r = lax.rsqrt(jnp.mean(x * x, axis=-1, keepdims=True) + NORM_EPS)
        h_ref[r0:r0 + sub, :] = ((x * r) * gain + shift).astype(jnp.bfloat16)

    def project(r0, k):
        return jnp.dot(h_ref[r0:r0 + sub, :], win_ref[:, k * blk:(k + 1) * blk],
                       preferred_element_type=jnp.float32)

    def lagged(e, lag):
        return pltpu.roll(e, lag, axis=0)[HALO:, :]

    def mix_conv(r0, d, res):
        cols = slice(d * gd, (d + 1) * gd)
        u, c, b, gate = (res[:, k * gd:(k + 1) * gd] for k in range(4))
        z = c * u
        ez = jnp.concatenate([zc_ref[:, cols], z], axis=0)
        zc_ref[:, cols] = z[sub - HALO:, :]
        conv = wconv_ref[CONV_K - 1:CONV_K, cols] * z
        for k in range(CONV_K - 1):
            conv = conv + wconv_ref[k:k + 1, cols] * lagged(ez, CONV_K - 1 - k)
        y_ref[r0:r0 + sub, cols] = (b * conv * _silu(gate)).astype(jnp.bfloat16)

    def mix_pool(r0, pair, res):
        for k in range(2):
            g = 2 * pair + k
            w = POOL_WINDOWS[g]
            cols = slice(g * gd, (g + 1) * gd)
            v = res[:, k * gd:(k + 1) * gd]
            gate = res[:, (2 + k) * gd:(3 + k) * gd]
            s = jnp.concatenate([pc_ref[:, cols], v], axis=0)
            pc_ref[:, cols] = v[sub - HALO:, :]
            span = 1
            while span < min(w, 8):
                s = s + pltpu.roll(s, span, axis=0)
                span *= 2
            if span < w:
                s = s[8:, :] + s[:-8, :]
                s = s[HALO - 8:, :]
            else:
                s = s[HALO:, :]
            if r0 == 0:
                pos = (lax.broadcasted_iota(jnp.int32, (HALO, gd), 0) + (j * tt + 1)).astype(jnp.float32)
                inv = jnp.concatenate([1.0 / jnp.minimum(pos, float(w)),
                                       jnp.full((sub - HALO, gd), 1.0 / w, jnp.float32)], axis=0)
            else:
                inv = 1.0 / w
            y_ref[r0:r0 + sub, cw + g * gd:cw + (g + 1) * gd] = (
                (s * inv - v) * _silu(gate)).astype(jnp.bfloat16)

    def finish(r0):
        rows = slice(r0, r0 + sub)
        yo = jnp.dot(y_ref[rows, :], wout_ref[...], preferred_element_type=jnp.float32)
        r2 = lax.rsqrt(jnp.mean(yo * yo, axis=-1, keepdims=True) + NORM_EPS)
        o_ref[rows, :] = x_ref[rows, :] + (yo * r2) * out_gain

    assert n_conv >= 2 and n_pair == 2
    norm(0)
    head = [project(0, 0), project(0, 1)]
    for r0 in range(0, tt, sub):
        nxt = r0 + sub if r0 + sub < tt else None
        if nxt is not None:
            norm(nxt)
        res = head
        for d in range(n_conv):
            if d + 2 < n_conv + n_pair:
                res.append(project(r0, d + 2))
            mix_conv(r0, d, res[d])
        head = []
        for p in range(n_pair):
            mix_pool(r0, p, res[n_conv + p])
            if nxt is not None:
                head.append(project(nxt, p))
        finish(r0)


def _layer(l, x, mod, g_pre, g_post, w_in, w_conv, w_pool, pool_scale, w_out):
    batch, seq, d = x.shape
    cw = w_conv.shape[2]
    _, groups, gd, _ = w_pool.shape
    pw = groups * gd
    tt = SEQ_TILE
    per_layer = lambda b, j: (l, 0, 0)
    assert w_in.shape[1] == w_out.shape[1] and w_in.shape[1] % WEIGHT_CAST_ROWS == 0
    assert cw == pw and cw % gd == 0 and groups % 2 == 0
    once = pl.Buffered(1)
    return pl.pallas_call(
        functools.partial(_layer_kernel, cw=cw, gd=gd, sub=SUB_ROWS),
        grid=(batch, seq // tt),
        in_specs=[
            pl.BlockSpec((None, tt, d), lambda b, j: (b, j, 0)),
            pl.BlockSpec((None, None, 3, d), lambda b, j: (l, b, 0, 0)),
            pl.BlockSpec((None, 1, d), per_layer),
            pl.BlockSpec((None, 1, d), per_layer),
            pl.BlockSpec((None,) + w_in.shape[1:], per_layer, pipeline_mode=once),
            pl.BlockSpec((None,) + w_conv.shape[1:], per_layer),
            pl.BlockSpec((None,) + w_pool.shape[1:], lambda b, j: (l, 0, 0, 0)),
            pl.BlockSpec((None, 1, pw), per_layer),
            pl.BlockSpec((None,) + w_out.shape[1:], per_layer, pipeline_mode=once),
        ],
        out_specs=pl.BlockSpec((None, tt, d), lambda b, j: (b, j, 0)),
        out_shape=jax.ShapeDtypeStruct(x.shape, x.dtype),
        scratch_shapes=[
            pltpu.VMEM(w_in.shape[1:], jnp.bfloat16),
            pltpu.VMEM(w_out.shape[1:], jnp.bfloat16),
            pltpu.VMEM((HALO, cw), jnp.float32),
            pltpu.VMEM((HALO, pw), jnp.float32),
            pltpu.VMEM((groups // 2, 2 * gd, 2 * gd), jnp.bfloat16),
            pltpu.VMEM((tt, d), jnp.bfloat16),
            pltpu.VMEM((tt, cw + pw), jnp.bfloat16),
        ],
        compiler_params=pltpu.CompilerParams(
            dimension_semantics=("arbitrary", "arbitrary"),
            vmem_limit_bytes=VMEM_LIMIT_BYTES),
        name="mixer_layer",
    )(x, mod, g_pre, g_post, w_in, w_conv, w_pool, pool_scale, w_out)


def kernel(x, c, w_ada, b_ada, g_pre, w_in, w_conv, w_pool, pool_scale, w_out, g_post):
    depth = w_ada.shape[0]
    batch, _, d = x.shape
    mod = _ada_modulation(c, w_ada, b_ada).reshape(depth, batch, 3, d)
    g_pre3 = g_pre.reshape(depth, 1, d)
    g_post3 = g_post.reshape(depth, 1, d)
    pool_scale3 = pool_scale.reshape(depth, 1, -1)
    for l in range(depth):
        x = _layer(l, x, mod, g_pre3, g_post3, w_in, w_conv, w_pool, pool_scale3, w_out)
    return x
```

```python
import functools

import jax
import jax.numpy as jnp
from jax import lax
from jax.experimental import pallas as pl
from jax.experimental.pallas import tpu as pltpu

CONV_K = 3
POOL_WINDOWS = (2, 4, 8, 16)
NORM_EPS = 1e-6
HALO = 16
SEQ_TILE = 1024
SUB_ROWS = 256
ADA_COL_TILE = 1024
WEIGHT_CAST_ROWS = 128
VMEM_LIMIT_BYTES = 56 * 1024 * 1024


def _silu(v):
    return v * jax.nn.sigmoid(v)


def _ada_kernel(c_ref, w_ref, b_ref, o_ref):
    c_act = _silu(c_ref[...]).astype(jnp.bfloat16)
    w = w_ref[...].astype(jnp.bfloat16)
    o_ref[...] = jnp.dot(c_act, w, preferred_element_type=jnp.float32) + b_ref[...]


def _ada_modulation(c, w_ada, b_ada):
    depth, d, cols = w_ada.shape
    batch = c.shape[0]
    tn = ADA_COL_TILE
    return pl.pallas_call(
        _ada_kernel,
        grid=(depth, cols // tn),
        in_specs=[
            pl.BlockSpec((batch, d), lambda l, n: (0, 0)),
            pl.BlockSpec((None, d, tn), lambda l, n: (l, 0, n)),
            pl.BlockSpec((None, 1, tn), lambda l, n: (l, 0, n)),
        ],
        out_specs=pl.BlockSpec((None, batch, tn), lambda l, n: (l, 0, n)),
        out_shape=jax.ShapeDtypeStruct((depth, batch, cols), jnp.float32),
        compiler_params=pltpu.CompilerParams(
            dimension_semantics=("arbitrary", "arbitrary"),
            vmem_limit_bytes=VMEM_LIMIT_BYTES),
        name="ada_modulation",
    )(c, w_ada, b_ada.reshape(depth, 1, cols))


def _proj_block_order(cw, pw, gd):
    nb = cw // gd
    order = []
    for d in range(nb):
        order += [0 * nb + d, 2 * nb + d, 1 * nb + d, 3 * nb + d]
    for g in range(0, pw // gd, 2):
        order += [4 * nb + g, 4 * nb + g + 1, 4 * nb + pw // gd + g, 4 * nb + pw // gd + g + 1]
    return order


def _layer_kernel(x_ref, xn_ref, mod_ref, modn_ref, gpre_ref, gpost_ref, win32_ref, wconv_ref, wpool_ref,
                  pscale_ref, wout32_ref, o_ref, win_ref, wout_ref, zc_ref, pc_ref, wpair_ref,
                  h_ref, y_ref, head_ref, *, cw, gd, sub):
    j = pl.program_id(1)
    tt = x_ref.shape[0]
    n_pool = len(POOL_WINDOWS)
    pw = gd * n_pool
    n_conv = cw // gd
    n_pair = n_pool // 2
    blk = 4 * gd
    order = _proj_block_order(cw, pw, gd)

    @pl.when(jnp.logical_and(pl.program_id(0) == 0, j == 0))
    def _():
        zero = jnp.zeros((gd, gd), jnp.float32)
        for g in range(0, n_pool, 2):
            scaled = [wpool_ref[g + k] * pscale_ref[:, (g + k) * gd:(g + k + 1) * gd] for k in range(2)]
            wpair_ref[g // 2] = jnp.concatenate(
                [jnp.concatenate([scaled[0], zero], axis=-1),
                 jnp.concatenate([zero, scaled[1]], axis=-1)], axis=0).astype(jnp.bfloat16)

        def cast_rows(i, carry):
            rows = pl.ds(pl.multiple_of(i * WEIGHT_CAST_ROWS, WEIGHT_CAST_ROWS), WEIGHT_CAST_ROWS)
            for q, p in enumerate(order):
                win_ref[rows, q * gd:(q + 1) * gd] = win32_ref[rows, p * gd:(p + 1) * gd].astype(jnp.bfloat16)
            for pair in range(n_pair):
                dst = slice((n_conv + pair) * blk, (n_conv + pair) * blk + 2 * gd)
                win_ref[rows, dst] = jnp.dot(win_ref[rows, dst], wpair_ref[pair],
                                             preferred_element_type=jnp.float32).astype(jnp.bfloat16)
            wout_ref[rows, :] = wout32_ref[rows, :].astype(jnp.bfloat16)
            return carry
        lax.fori_loop(0, win_ref.shape[0] // WEIGHT_CAST_ROWS, cast_rows, 0)

    def norm(src_ref, r0, m_ref):
        x = src_ref[r0:r0 + sub, :]
        r = lax.rsqrt(jnp.mean(x * x, axis=-1, keepdims=True) + NORM_EPS)
        gain = gpre_ref[...] * (1.0 + m_ref[1:2, :])
        hb = ((x * r) * gain + m_ref[0:1, :]).astype(jnp.bfloat16)
        h_ref[r0:r0 + sub, :] = hb
        bits = pltpu.bitcast(hb, jnp.uint32)
        zeros = pltpu.bitcast((bits >> 16) >> 16, jnp.float32)
        return jnp.concatenate([zeros[:, k:k + cw] for k in range(0, zeros.shape[1], cw)], axis=0)

    def project(r0, k):
        return jnp.dot(h_ref[r0:r0 + sub, :], win_ref[:, k * blk:(k + 1) * blk],
                       preferred_element_type=jnp.float32)

    @pl.when(jnp.logical_and(pl.program_id(0) == 0, j == 0))
    def _():
        norm(x_ref, 0, mod_ref)
        for p in range(2):
            head_ref[p] = project(0, p)

    @pl.when(j == 0)
    def _():
        zc_ref[...] = jnp.zeros((HALO, cw), jnp.float32)
        pc_ref[...] = jnp.zeros((HALO, pw), jnp.float32)

    out_gain = mod_ref[2:3, :] * gpost_ref[...]

    def lagged(e, lag):
        return pltpu.roll(e, lag, axis=0)[HALO:, :]

    def mix_conv(r0, d, res, ahead):
        cols = slice(d * gd, (d + 1) * gd)
        u, c, b, gate = (res[:, k * gd:(k + 1) * gd] for k in range(4))
        z = c * u
        if ahead is not None:
            z = z + ahead[:, cols]
        ez = jnp.concatenate([zc_ref[:, cols], z], axis=0)
        zc_ref[:, cols] = z[sub - HALO:, :]
        conv = wconv_ref[CONV_K - 1:CONV_K, cols] * z
        for k in range(CONV_K - 1):
            conv = conv + wconv_ref[k:k + 1, cols] * lagged(ez, CONV_K - 1 - k)
        y_ref[r0:r0 + sub, cols] = (b * conv * _silu(gate)).astype(jnp.bfloat16)

    def mix_pool(r0, pair, res):
        for k in range(2):
            g = 2 * pair + k
            w = POOL_WINDOWS[g]
            cols = slice(g * gd, (g + 1) * gd)
            v = res[:, k * gd:(k + 1) * gd]
            gate = res[:, (2 + k) * gd:(3 + k) * gd]
            s = jnp.concatenate([pc_ref[:, cols], v], axis=0)
            pc_ref[:, cols] = v[sub - HALO:, :]
            span = 1
            while span < min(w, 8):
                s = s + pltpu.roll(s, span, axis=0)
                span *= 2
            if span < w:
                s = s[8:, :] + s[:-8, :]
                s = s[HALO - 8:, :]
            else:
                s = s[HALO:, :]
            if r0 == 0:
                pos = (lax.broadcasted_iota(jnp.int32, (HALO, gd), 0) + (j * tt + 1)).astype(jnp.float32)
                inv = jnp.concatenate([1.0 / jnp.minimum(pos, float(w)),
                                       jnp.full((sub - HALO, gd), 1.0 / w, jnp.float32)], axis=0)
            else:
                inv = 1.0 / w
            y_ref[r0:r0 + sub, cw + g * gd:cw + (g + 1) * gd] = (
                (s * inv - v) * _silu(gate)).astype(jnp.bfloat16)

    def finish(r0):
        rows = slice(r0, r0 + sub)
        yo = jnp.dot(y_ref[rows, :], wout_ref[...], preferred_element_type=jnp.float32)
        r2 = lax.rsqrt(jnp.mean(yo * yo, axis=-1, keepdims=True) + NORM_EPS)
        o_ref[rows, :] = x_ref[rows, :] + (yo * r2) * out_gain

    assert n_conv >= 2 and n_pair == 2
    head = [head_ref[0], head_ref[1]]
    for r0 in range(0, tt, sub):
        nxt = r0 + sub if r0 + sub < tt else None
        ahead = norm(x_ref, nxt, mod_ref) if nxt is not None else norm(xn_ref, 0, modn_ref)
        res = head
        for d in range(n_conv):
            if d + 2 < n_conv + n_pair:
                res.append(project(r0, d + 2))
            mix_conv(r0, d, res[d], ahead)
        head = []
        for p in range(n_pair):
            mix_pool(r0, p, res[n_conv + p])
            if nxt is not None:
                head.append(project(nxt, p))
            else:
                head_ref[p] = project(0, p)
        finish(r0)


def _layer(l, x, mod, g_pre, g_post, w_in, w_conv, w_pool, pool_scale, w_out):
    batch, seq, d = x.shape
    cw = w_conv.shape[2]
    _, groups, gd, _ = w_pool.shape
    pw = groups * gd
    tt = SEQ_TILE
    per_layer = lambda b, j: (l, 0, 0)
    n_tiles = seq // tt

    def next_tile(b, j):
        f = jnp.minimum(b * n_tiles + j + 1, batch * n_tiles - 1)
        return f // n_tiles, f % n_tiles

    def next_chain(b, j):
        nb, nj = next_tile(b, j)
        return nb, nj * (tt // SUB_ROWS), 0

    assert w_in.shape[1] == w_out.shape[1] and w_in.shape[1] % WEIGHT_CAST_ROWS == 0
    assert cw == pw and cw % gd == 0 and groups % 2 == 0
    once = pl.Buffered(1)
    return pl.pallas_call(
        functools.partial(_layer_kernel, cw=cw, gd=gd, sub=SUB_ROWS),
        grid=(batch, seq // tt),
        in_specs=[
            pl.BlockSpec((None, tt, d), lambda b, j: (b, j, 0)),
            pl.BlockSpec((None, SUB_ROWS, d), next_chain),
            pl.BlockSpec((None, None, 3, d), lambda b, j: (l, b, 0, 0)),
            pl.BlockSpec((None, None, 3, d), lambda b, j: (l, next_tile(b, j)[0], 0, 0)),
            pl.BlockSpec((None, 1, d), per_layer),
            pl.BlockSpec((None, 1, d), per_layer),
            pl.BlockSpec((None,) + w_in.shape[1:], per_layer, pipeline_mode=once),
            pl.BlockSpec((None,) + w_conv.shape[1:], per_layer),
            pl.BlockSpec((None,) + w_pool.shape[1:], lambda b, j: (l, 0, 0, 0)),
            pl.BlockSpec((None, 1, pw), per_layer),
            pl.BlockSpec((None,) + w_out.shape[1:], per_layer, pipeline_mode=once),
        ],
        out_specs=pl.BlockSpec((None, tt, d), lambda b, j: (b, j, 0)),
        out_shape=jax.ShapeDtypeStruct(x.shape, x.dtype),
        scratch_shapes=[
            pltpu.VMEM(w_in.shape[1:], jnp.bfloat16),
            pltpu.VMEM(w_out.shape[1:], jnp.bfloat16),
            pltpu.VMEM((HALO, cw), jnp.float32),
            pltpu.VMEM((HALO, pw), jnp.float32),
            pltpu.VMEM((groups // 2, 2 * gd, 2 * gd), jnp.bfloat16),
            pltpu.VMEM((tt, d), jnp.bfloat16),
            pltpu.VMEM((tt, cw + pw), jnp.bfloat16),
            pltpu.VMEM((2, SUB_ROWS, 4 * gd), jnp.float32),
        ],
        compiler_params=pltpu.CompilerParams(
            dimension_semantics=("arbitrary", "arbitrary"),
            vmem_limit_bytes=VMEM_LIMIT_BYTES),
        name="mixer_layer",
    )(x, x, mod, mod, g_pre, g_post, w_in, w_conv, w_pool, pool_scale, w_out)


def kernel(x, c, w_ada, b_ada, g_pre, w_in, w_conv, w_pool, pool_scale, w_out, g_post):
    depth = w_ada.shape[0]
    batch, _, d = x.shape
    mod = _ada_modulation(c, w_ada, b_ada).reshape(depth, batch, 3, d)
    g_pre3 = g_pre.reshape(depth, 1, d)
    g_post3 = g_post.reshape(depth, 1, d)
    pool_scale3 = pool_scale.reshape(depth, 1, -1)
    for l in range(depth):
        x = _layer(l, x, mod, g_pre3, g_post3, w_in, w_conv, w_pool, pool_scale3, w_out)
    return x
```

```python
import functools

import jax
import jax.numpy as jnp
from jax import lax
from jax.experimental import pallas as pl
from jax.experimental.pallas import tpu as pltpu

CONV_K = 3
POOL_WINDOWS = (2, 4, 8, 16)
NORM_EPS = 1e-6
HALO = 16
SEQ_TILE = 1024
SUB_ROWS = 256
WEIGHT_CAST_ROWS = 128
VMEM_LIMIT_BYTES = 56 * 1024 * 1024


def _silu(v):
    return v * jax.nn.sigmoid(v)


def _ada_kernel(c_ref, w_ref, b_ref, o_ref):
    l = pl.program_id(0)
    c_act = _silu(c_ref[...]).astype(jnp.bfloat16)
    w = w_ref[...].astype(jnp.bfloat16)
    o_ref[...] = jnp.dot(c_act, w, preferred_element_type=jnp.float32) + b_ref[pl.ds(l, 1), :]


def _ada_modulation(c, w_ada, b_ada):
    depth, d, cols = w_ada.shape
    batch = c.shape[0]
    return pl.pallas_call(
        _ada_kernel,
        grid=(depth,),
        in_specs=[
            pl.BlockSpec((batch, d), lambda l: (0, 0)),
            pl.BlockSpec((None, d, cols), lambda l: (l, 0, 0)),
            pl.BlockSpec((depth, cols), lambda l: (0, 0)),
        ],
        out_specs=pl.BlockSpec((None, batch, cols), lambda l: (l, 0, 0)),
        out_shape=jax.ShapeDtypeStruct((depth, batch, cols), jnp.float32),
        compiler_params=pltpu.CompilerParams(
            dimension_semantics=("arbitrary",),
            vmem_limit_bytes=VMEM_LIMIT_BYTES),
        name="ada_modulation",
    )(c, w_ada, b_ada)


def _proj_block_order(cw, pw, gd):
    nb = cw // gd
    order = []
    for d in range(nb):
        order += [0 * nb + d, 2 * nb + d, 1 * nb + d, 3 * nb + d]
    for g in range(0, pw // gd, 2):
        order += [4 * nb + g, 4 * nb + g + 1, 4 * nb + pw // gd + g, 4 * nb + pw // gd + g + 1]
    return order


def _layer_kernel(x_ref, mod_ref, gpre_ref, gpost_ref, win32_ref, wconv_ref, wpool_ref,
                  pscale_ref, wout32_ref, o_ref, win_ref, wout_ref, zc_ref, pc_ref, wpair_ref,
                  h_ref, y_ref, *, layer, cw, gd, sub):
    b = pl.program_id(0)
    j = pl.program_id(1)
    tt, dm = x_ref.shape
    n_pool = len(POOL_WINDOWS)
    pw = gd * n_pool
    n_conv = cw // gd
    n_pair = n_pool // 2
    blk = 4 * gd
    order = _proj_block_order(cw, pw, gd)
    this_layer = slice(layer, layer + 1)

    @pl.when(jnp.logical_and(b == 0, j == 0))
    def _():
        zero = jnp.zeros((gd, gd), jnp.float32)
        for g in range(0, n_pool, 2):
            scaled = [wpool_ref[g + k] * pscale_ref[this_layer, (g + k) * gd:(g + k + 1) * gd]
                      for k in range(2)]
            wpair_ref[g // 2] = jnp.concatenate(
                [jnp.concatenate([scaled[0], zero], axis=-1),
                 jnp.concatenate([zero, scaled[1]], axis=-1)], axis=0).astype(jnp.bfloat16)

        def cast_rows(i, carry):
            rows = pl.ds(pl.multiple_of(i * WEIGHT_CAST_ROWS, WEIGHT_CAST_ROWS), WEIGHT_CAST_ROWS)
            for q, p in enumerate(order):
                win_ref[rows, q * gd:(q + 1) * gd] = win32_ref[rows, p * gd:(p + 1) * gd].astype(jnp.bfloat16)
            for pair in range(n_pair):
                dst = slice((n_conv + pair) * blk, (n_conv + pair) * blk + 2 * gd)
                win_ref[rows, dst] = jnp.dot(win_ref[rows, dst], wpair_ref[pair],
                                             preferred_element_type=jnp.float32).astype(jnp.bfloat16)
            wout_ref[rows, :] = wout32_ref[rows, :].astype(jnp.bfloat16)
            return carry
        lax.fori_loop(0, win_ref.shape[0] // WEIGHT_CAST_ROWS, cast_rows, 0)

    @pl.when(j == 0)
    def _():
        zc_ref[...] = jnp.zeros((HALO, cw), jnp.float32)
        pc_ref[...] = jnp.zeros((HALO, pw), jnp.float32)

    mod = mod_ref[pl.ds(b, 1), :]
    shift = mod[:, 0:dm]
    gain = gpre_ref[this_layer, :] * (1.0 + mod[:, dm:2 * dm])
    out_gain = mod[:, 2 * dm:3 * dm] * gpost_ref[this_layer, :]

    def norm(r0):
        x = x_ref[r0:r0 + sub, :]
        r = lax.rsqrt(jnp.mean(x * x, axis=-1, keepdims=True) + NORM_EPS)
        h_ref[r0:r0 + sub, :] = ((x * r) * gain + shift).astype(jnp.bfloat16)

    def project(r0, k):
        return jnp.dot(h_ref[r0:r0 + sub, :], win_ref[:, k * blk:(k + 1) * blk],
                       preferred_element_type=jnp.float32)

    def lagged(e, lag):
        return pltpu.roll(e, lag, axis=0)[HALO:, :]

    def mix_conv(r0, d, res):
        cols = slice(d * gd, (d + 1) * gd)
        u, c, b_a, gate = (res[:, k * gd:(k + 1) * gd] for k in range(4))
        z = c * u
        ez = jnp.concatenate([zc_ref[:, cols], z], axis=0)
        zc_ref[:, cols] = z[sub - HALO:, :]
        conv = wconv_ref[CONV_K - 1:CONV_K, cols] * z
        for k in range(CONV_K - 1):
            conv = conv + wconv_ref[k:k + 1, cols] * lagged(ez, CONV_K - 1 - k)
        y_ref[r0:r0 + sub, cols] = (b_a * conv * _silu(gate)).astype(jnp.bfloat16)

    def mix_pool(r0, pair, res):
        for k in range(2):
            g = 2 * pair + k
            w = POOL_WINDOWS[g]
            cols = slice(g * gd, (g + 1) * gd)
            v = res[:, k * gd:(k + 1) * gd]
            gate = res[:, (2 + k) * gd:(3 + k) * gd]
            s = jnp.concatenate([pc_ref[:, cols], v], axis=0)
            pc_ref[:, cols] = v[sub - HALO:, :]
            span = 1
            while span < min(w, 8):
                s = s + pltpu.roll(s, span, axis=0)
                span *= 2
            if span < w:
                s = s[8:, :] + s[:-8, :]
                s = s[HALO - 8:, :]
            else:
                s = s[HALO:, :]
            if r0 == 0:
                pos = (lax.broadcasted_iota(jnp.int32, (HALO, gd), 0) + (j * tt + 1)).astype(jnp.float32)
                inv = jnp.concatenate([1.0 / jnp.minimum(pos, float(w)),
                                       jnp.full((sub - HALO, gd), 1.0 / w, jnp.float32)], axis=0)
            else:
                inv = 1.0 / w
            y_ref[r0:r0 + sub, cw + g * gd:cw + (g + 1) * gd] = (
                (s * inv - v) * _silu(gate)).astype(jnp.bfloat16)

    def finish(r0):
        rows = slice(r0, r0 + sub)
        yo = jnp.dot(y_ref[rows, :], wout_ref[...], preferred_element_type=jnp.float32)
        r2 = lax.rsqrt(jnp.mean(yo * yo, axis=-1, keepdims=True) + NORM_EPS)
        o_ref[rows, :] = x_ref[rows, :] + (yo * r2) * out_gain

    assert n_conv >= 2 and n_pair == 2
    norm(0)
    head = [project(0, 0), project(0, 1)]
    for r0 in range(0, tt, sub):
        nxt = r0 + sub if r0 + sub < tt else None
        if nxt is not None:
            norm(nxt)
        res = head
        for d in range(n_conv):
            if d + 2 < n_conv + n_pair:
                res.append(project(r0, d + 2))
            mix_conv(r0, d, res[d])
        head = []
        for p in range(n_pair):
            mix_pool(r0, p, res[n_conv + p])
            if nxt is not None:
                head.append(project(nxt, p))
        finish(r0)


def _layer(l, x, mod, g_pre, g_post, w_in, w_conv, w_pool, pool_scale, w_out):
    batch, seq, d = x.shape
    depth = w_in.shape[0]
    cw = w_conv.shape[2]
    _, groups, gd, _ = w_pool.shape
    pw = groups * gd
    tt = SEQ_TILE
    per_layer = lambda b, j: (l, 0, 0)
    whole = lambda b, j: (0, 0)
    assert w_in.shape[1] == w_out.shape[1] and w_in.shape[1] % WEIGHT_CAST_ROWS == 0
    assert cw == pw and cw % gd == 0 and groups % 2 == 0
    once = pl.Buffered(1)
    return pl.pallas_call(
        functools.partial(_layer_kernel, layer=l, cw=cw, gd=gd, sub=SUB_ROWS),
        grid=(batch, seq // tt),
        in_specs=[
            pl.BlockSpec((None, tt, d), lambda b, j: (b, j, 0)),
            pl.BlockSpec((None, batch, 3 * d), per_layer),
            pl.BlockSpec((depth, d), whole),
            pl.BlockSpec((depth, d), whole),
            pl.BlockSpec((None,) + w_in.shape[1:], per_layer, pipeline_mode=once),
            pl.BlockSpec((None,) + w_conv.shape[1:], per_layer),
            pl.BlockSpec((None,) + w_pool.shape[1:], lambda b, j: (l, 0, 0, 0)),
            pl.BlockSpec((depth, pw), whole),
            pl.BlockSpec((None,) + w_out.shape[1:], per_layer, pipeline_mode=once),
        ],
        out_specs=pl.BlockSpec((None, tt, d), lambda b, j: (b, j, 0)),
        out_shape=jax.ShapeDtypeStruct(x.shape, x.dtype),
        scratch_shapes=[
            pltpu.VMEM(w_in.shape[1:], jnp.bfloat16),
            pltpu.VMEM(w_out.shape[1:], jnp.bfloat16),
            pltpu.VMEM((HALO, cw), jnp.float32),
            pltpu.VMEM((HALO, pw), jnp.float32),
            pltpu.VMEM((groups // 2, 2 * gd, 2 * gd), jnp.bfloat16),
            pltpu.VMEM((tt, d), jnp.bfloat16),
            pltpu.VMEM((tt, cw + pw), jnp.bfloat16),
        ],
        compiler_params=pltpu.CompilerParams(
            dimension_semantics=("arbitrary", "arbitrary"),
            vmem_limit_bytes=VMEM_LIMIT_BYTES),
        name="mixer_layer",
    )(x, mod, g_pre, g_post, w_in, w_conv, w_pool, pool_scale, w_out)


def kernel(x, c, w_ada, b_ada, g_pre, w_in, w_conv, w_pool, pool_scale, w_out, g_post):
    mod = _ada_modulation(c, w_ada, b_ada)
    for l in range(w_ada.shape[0]):
        x = _layer(l, x, mod, g_pre, g_post, w_in, w_conv, w_pool, pool_scale, w_out)
    return x
```

```python
import functools

import jax
import jax.numpy as jnp
from jax import lax
from jax.experimental import pallas as pl
from jax.experimental.pallas import tpu as pltpu

CONV_K = 3
POOL_WINDOWS = (2, 4, 8, 16)
NORM_EPS = 1e-6
HALO = 16
SEQ_TILE = 1024
SUB_ROWS = 256
WEIGHT_CAST_ROWS = 128
VMEM_LIMIT_BYTES = 56 * 1024 * 1024


def _silu(v):
    return v * jax.nn.sigmoid(v)


def _ada_kernel(c_ref, w_ref, b_ref, o_ref):
    l = pl.program_id(0)
    c_act = _silu(c_ref[...]).astype(jnp.bfloat16)
    w = w_ref[...].astype(jnp.bfloat16)
    o_ref[...] = jnp.dot(c_act, w, preferred_element_type=jnp.float32) + b_ref[pl.ds(l, 1), :]


def _ada_modulation(c, w_ada, b_ada):
    depth, d, cols = w_ada.shape
    batch = c.shape[0]
    return pl.pallas_call(
        _ada_kernel,
        grid=(depth,),
        in_specs=[
            pl.BlockSpec((batch, d), lambda l: (0, 0)),
            pl.BlockSpec((None, d, cols), lambda l: (l, 0, 0)),
            pl.BlockSpec((depth, cols), lambda l: (0, 0)),
        ],
        out_specs=pl.BlockSpec((None, batch, cols), lambda l: (l, 0, 0)),
        out_shape=jax.ShapeDtypeStruct((depth, batch, cols), jnp.float32),
        compiler_params=pltpu.CompilerParams(
            dimension_semantics=("arbitrary",),
            vmem_limit_bytes=VMEM_LIMIT_BYTES),
        name="ada_modulation",
    )(c, w_ada, b_ada)


def _proj_block_order(cw, pw, gd):
    nb = cw // gd
    order = []
    for d in range(nb):
        order += [0 * nb + d, 2 * nb + d, 1 * nb + d, 3 * nb + d]
    for g in range(0, pw // gd, 2):
        order += [4 * nb + g, 4 * nb + g + 1, 4 * nb + pw // gd + g, 4 * nb + pw // gd + g + 1]
    return order


def _layer_kernel(x_ref, mod_ref, gpre_ref, gpost_ref, win32_ref, wconv_ref, wpool_ref,
                  pscale_ref, wout32_ref, o_ref, win_ref, wout_ref, zc_ref, pc_ref, wpair_ref,
                  h_ref, y_ref, *, layer, cw, gd, sub):
    b = pl.program_id(0)
    j = pl.program_id(1)
    tt, dm = x_ref.shape
    n_pool = len(POOL_WINDOWS)
    pw = gd * n_pool
    n_conv = cw // gd
    n_pair = n_pool // 2
    blk = 4 * gd
    order = _proj_block_order(cw, pw, gd)
    this_layer = slice(layer, layer + 1)

    @pl.when(jnp.logical_and(b == 0, j == 0))
    def _():
        zero = jnp.zeros((gd, gd), jnp.float32)
        for g in range(0, n_pool, 2):
            scaled = [wpool_ref[g + k] * pscale_ref[this_layer, (g + k) * gd:(g + k + 1) * gd]
                      for k in range(2)]
            wpair_ref[g // 2] = jnp.concatenate(
                [jnp.concatenate([scaled[0], zero], axis=-1),
                 jnp.concatenate([zero, scaled[1]], axis=-1)], axis=0).astype(jnp.bfloat16)

        def cast_rows(i, carry):
            rows = pl.ds(pl.multiple_of(i * WEIGHT_CAST_ROWS, WEIGHT_CAST_ROWS), WEIGHT_CAST_ROWS)
            for q, p in enumerate(order):
                win_ref[rows, q * gd:(q + 1) * gd] = win32_ref[rows, p * gd:(p + 1) * gd].astype(jnp.bfloat16)
            wout_ref[rows, :] = wout32_ref[rows, :].astype(jnp.bfloat16)
            return carry
        lax.fori_loop(0, win_ref.shape[0] // WEIGHT_CAST_ROWS, cast_rows, 0)

    @pl.when(j == 0)
    def _():
        zc_ref[...] = jnp.zeros((HALO, cw), jnp.float32)
        pc_ref[...] = jnp.zeros((HALO, pw), jnp.float32)

    mod = mod_ref[pl.ds(b, 1), :]
    shift = mod[:, 0:dm]
    gain = gpre_ref[this_layer, :] * (1.0 + mod[:, dm:2 * dm])
    out_gain = mod[:, 2 * dm:3 * dm] * gpost_ref[this_layer, :]

    def norm(r0):
        x = x_ref[r0:r0 + sub, :]
        r = lax.rsqrt(jnp.mean(x * x, axis=-1, keepdims=True) + NORM_EPS)
        h_ref[r0:r0 + sub, :] = ((x * r) * gain + shift).astype(jnp.bfloat16)

    def project(r0, k):
        return jnp.dot(h_ref[r0:r0 + sub, :], win_ref[:, k * blk:(k + 1) * blk],
                       preferred_element_type=jnp.float32)

    def lagged(e, lag):
        return pltpu.roll(e, lag, axis=0)[HALO:, :]

    def mix_conv(r0, d, res):
        cols = slice(d * gd, (d + 1) * gd)
        u, c, b_a, gate = (res[:, k * gd:(k + 1) * gd] for k in range(4))
        z = c * u
        ez = jnp.concatenate([zc_ref[:, cols], z], axis=0)
        zc_ref[:, cols] = z[sub - HALO:, :]
        conv = wconv_ref[CONV_K - 1:CONV_K, cols] * z
        for k in range(CONV_K - 1):
            conv = conv + wconv_ref[k:k + 1, cols] * lagged(ez, CONV_K - 1 - k)
        y_ref[r0:r0 + sub, cols] = (b_a * conv * _silu(gate)).astype(jnp.bfloat16)

    def mix_pool(r0, pair, res):
        pooled = []
        for k in range(2):
            g = 2 * pair + k
            w = POOL_WINDOWS[g]
            cols = slice(g * gd, (g + 1) * gd)
            v = res[:, k * gd:(k + 1) * gd]
            s = jnp.concatenate([pc_ref[:, cols], v], axis=0)
            pc_ref[:, cols] = v[sub - HALO:, :]
            span = 1
            while span < min(w, 8):
                s = s + pltpu.roll(s, span, axis=0)
                span *= 2
            if span < w:
                s = s[8:, :] + s[:-8, :]
                s = s[HALO - 8:, :]
            else:
                s = s[HALO:, :]
            if r0 == 0:
                pos = (lax.broadcasted_iota(jnp.int32, (HALO, gd), 0) + (j * tt + 1)).astype(jnp.float32)
                inv = jnp.concatenate([1.0 / jnp.minimum(pos, float(w)),
                                       jnp.full((sub - HALO, gd), 1.0 / w, jnp.float32)], axis=0)
            else:
                inv = 1.0 / w
            pooled.append(s * inv - v)
        mapped = jnp.dot(jnp.concatenate(pooled, axis=-1).astype(jnp.bfloat16), wpair_ref[pair],
                         preferred_element_type=jnp.float32)
        return mapped, res[:, 2 * gd:4 * gd]

    def gate_pool(r0, pair, mapped, gate):
        y_ref[r0:r0 + sub, cw + pair * 2 * gd:cw + (pair + 1) * 2 * gd] = (
            mapped * _silu(gate)).astype(jnp.bfloat16)

    def finish(r0):
        rows = slice(r0, r0 + sub)
        yo = jnp.dot(y_ref[rows, :], wout_ref[...], preferred_element_type=jnp.float32)
        r2 = lax.rsqrt(jnp.mean(yo * yo, axis=-1, keepdims=True) + NORM_EPS)
        o_ref[rows, :] = x_ref[rows, :] + (yo * r2) * out_gain

    assert n_conv >= 2 and n_pair == 2
    norm(0)
    head = [project(0, 0), project(0, 1)]
    for r0 in range(0, tt, sub):
        nxt = r0 + sub if r0 + sub < tt else None
        if nxt is not None:
            norm(nxt)
        res = head
        for d in range(n_conv):
            if d + 2 < n_conv + n_pair:
                res.append(project(r0, d + 2))
            mix_conv(r0, d, res[d])
        head = []
        mapped = []
        for p in range(n_pair):
            mapped.append(mix_pool(r0, p, res[n_conv + p]))
            if nxt is not None:
                head.append(project(nxt, p))
        for p in range(n_pair):
            gate_pool(r0, p, *mapped[p])
        finish(r0)


def _layer(l, x, mod, g_pre, g_post, w_in, w_conv, w_pool, pool_scale, w_out):
    batch, seq, d = x.shape
    depth = w_in.shape[0]
    cw = w_conv.shape[2]
    _, groups, gd, _ = w_pool.shape
    pw = groups * gd
    tt = SEQ_TILE
    per_layer = lambda b, j: (l, 0, 0)
    whole = lambda b, j: (0, 0)
    assert w_in.shape[1] == w_out.shape[1] and w_in.shape[1] % WEIGHT_CAST_ROWS == 0
    assert cw == pw and cw % gd == 0 and groups % 2 == 0
    once = pl.Buffered(1)
    return pl.pallas_call(
        functools.partial(_layer_kernel, layer=l, cw=cw, gd=gd, sub=SUB_ROWS),
        grid=(batch, seq // tt),
        in_specs=[
            pl.BlockSpec((None, tt, d), lambda b, j: (b, j, 0)),
            pl.BlockSpec((None, batch, 3 * d), per_layer),
            pl.BlockSpec((depth, d), whole),
            pl.BlockSpec((depth, d), whole),
            pl.BlockSpec((None,) + w_in.shape[1:], per_layer, pipeline_mode=once),
            pl.BlockSpec((None,) + w_conv.shape[1:], per_layer),
            pl.BlockSpec((None,) + w_pool.shape[1:], lambda b, j: (l, 0, 0, 0)),
            pl.BlockSpec((depth, pw), whole),
            pl.BlockSpec((None,) + w_out.shape[1:], per_layer, pipeline_mode=once),
        ],
        out_specs=pl.BlockSpec((None, tt, d), lambda b, j: (b, j, 0)),
        out_shape=jax.ShapeDtypeStruct(x.shape, x.dtype),
        scratch_shapes=[
            pltpu.VMEM(w_in.shape[1:], jnp.bfloat16),
            pltpu.VMEM(w_out.shape[1:], jnp.bfloat16),
            pltpu.VMEM((HALO, cw), jnp.float32),
            pltpu.VMEM((HALO, pw), jnp.float32),
            pltpu.VMEM((groups // 2, 2 * gd, 2 * gd), jnp.bfloat16),
            pltpu.VMEM((tt, d), jnp.bfloat16),
            pltpu.VMEM((tt, cw + pw), jnp.bfloat16),
        ],
        compiler_params=pltpu.CompilerParams(
            dimension_semantics=("arbitrary", "arbitrary"),
            vmem_limit_bytes=VMEM_LIMIT_BYTES),
        name="mixer_layer",
    )(x, mod, g_pre, g_post, w_in, w_conv, w_pool, pool_scale, w_out)


def kernel(x, c, w_ada, b_ada, g_pre, w_in, w_conv, w_pool, pool_scale, w_out, g_post):
    mod = _ada_modulation(c, w_ada, b_ada)
    for l in range(w_ada.shape[0]):
        x = _layer(l, x, mod, g_pre, g_post, w_in, w_conv, w_pool, pool_scale, w_out)
    return x
```

```python
import functools

import jax
import jax.numpy as jnp
from jax import lax
from jax.experimental import pallas as pl
from jax.experimental.pallas import tpu as pltpu

CONV_K = 3
POOL_WINDOWS = (2, 4, 8, 16)
NORM_EPS = 1e-6
HALO = 16
SEQ_TILE = 1024
SUB_ROWS = 256
WEIGHT_CHUNK_ROWS = 128
VMEM_LIMIT_BYTES = 56 * 1024 * 1024


def _silu(v):
    return v * jax.nn.sigmoid(v)


def _pick_row(ref, idx):
    a = ref[...]
    rows = lax.broadcasted_iota(jnp.int32, a.shape, 0)
    return jnp.sum(jnp.where(rows == idx, a, 0.0), axis=0, keepdims=True)


def _ada_kernel(c_ref, w_ref, b_ref, o_ref):
    l = pl.program_id(0)
    c_act = _silu(c_ref[...]).astype(jnp.bfloat16)
    w = w_ref[...].astype(jnp.bfloat16)
    o_ref[...] = jnp.dot(c_act, w, preferred_element_type=jnp.float32) + b_ref[pl.ds(l, 1), :]


def _ada_modulation(c, w_ada, b_ada):
    depth, d, cols = w_ada.shape
    batch = c.shape[0]
    return pl.pallas_call(
        _ada_kernel,
        grid=(depth,),
        in_specs=[
            pl.BlockSpec((batch, d), lambda l: (0, 0)),
            pl.BlockSpec((None, d, cols), lambda l: (l, 0, 0)),
            pl.BlockSpec((depth, cols), lambda l: (0, 0)),
        ],
        out_specs=pl.BlockSpec((None, batch, cols), lambda l: (l, 0, 0)),
        out_shape=jax.ShapeDtypeStruct((depth, batch, cols), jnp.float32),
        compiler_params=pltpu.CompilerParams(
            dimension_semantics=("arbitrary",),
            vmem_limit_bytes=VMEM_LIMIT_BYTES),
        name="ada_modulation",
    )(c, w_ada, b_ada)


def _proj_block_order(cw, pw, gd):
    nb = cw // gd
    order = []
    for d in range(nb):
        order += [0 * nb + d, 2 * nb + d, 1 * nb + d, 3 * nb + d]
    for g in range(0, pw // gd, 2):
        order += [4 * nb + g, 4 * nb + g + 1, 4 * nb + pw // gd + g, 4 * nb + pw // gd + g + 1]
    return order


def _trunk_kernel(x_hbm, mod_ref, gpre_ref, gpost_ref, win_hbm, wconv_ref, wpool_ref, pscale_ref,
                  wout_hbm, o_hbm, xbuf, obuf, win_ref, wout_ref, wpair_ref, stage_in, stage_out,
                  zc_ref, pc_ref, h_ref, y_ref, sem_x, sem_o, sem_w, *, cw, gd, sub, n_tiles):
    l = pl.program_id(0)
    s = pl.program_id(1)
    depth = pl.num_programs(0)
    n_steps = pl.num_programs(1)
    _, tt, dm = xbuf.shape
    n_pool = len(POOL_WINDOWS)
    pw = gd * n_pool
    n_conv = cw // gd
    n_pair = n_pool // 2
    blk = 4 * gd
    order = _proj_block_order(cw, pw, gd)
    chunk = WEIGHT_CHUNK_ROWS
    n_chunks = win_hbm.shape[1] // chunk

    b = s // n_tiles
    j = s % n_tiles
    slot = s % 2
    wslot = l % 2
    step = l * n_steps + s
    first = step == 0
    last = step == depth * n_steps - 1

    def x_copy(src_hbm, bb, jj, sl):
        return pltpu.make_async_copy(src_hbm.at[bb, pl.ds(jj * tt, tt), :], xbuf.at[sl], sem_x.at[sl])

    def o_copy(bb, jj, sl):
        return pltpu.make_async_copy(obuf.at[sl], o_hbm.at[bb, pl.ds(jj * tt, tt), :], sem_o.at[sl])

    def w_copies(layer, k, st):
        rows = pl.ds(pl.multiple_of(k * chunk, chunk), chunk)
        return (pltpu.make_async_copy(win_hbm.at[layer, rows, :], stage_in.at[st], sem_w.at[0, st]),
                pltpu.make_async_copy(wout_hbm.at[layer, rows, :], stage_out.at[st], sem_w.at[1, st]))

    def cast_chunk(k, st, ws):
        rows = pl.ds(pl.multiple_of(k * chunk, chunk), chunk)
        for q, p in enumerate(order):
            win_ref[ws, rows, q * gd:(q + 1) * gd] = stage_in[st, :, p * gd:(p + 1) * gd].astype(jnp.bfloat16)
        wout_ref[ws, rows, :] = stage_out[st].astype(jnp.bfloat16)

    def make_pool_maps(layer, ws):
        zero = jnp.zeros((gd, gd), jnp.float32)
        scale = _pick_row(pscale_ref, layer)
        for g in range(0, n_pool, 2):
            scaled = [wpool_ref[layer, g + k] * scale[:, (g + k) * gd:(g + k + 1) * gd] for k in range(2)]
            wpair_ref[ws, g // 2] = jnp.concatenate(
                [jnp.concatenate([scaled[0], zero], axis=-1),
                 jnp.concatenate([zero, scaled[1]], axis=-1)], axis=0).astype(jnp.bfloat16)

    @pl.when(first)
    def _():
        x_copy(x_hbm, 0, 0, 0).start()
        make_pool_maps(0, 0)

        def fetch(k, carry):
            c_in, c_out = w_copies(0, k, 0)
            c_in.start()
            c_out.start()
            c_in.wait()
            c_out.wait()
            cast_chunk(k, 0, 0)
            return carry
        lax.fori_loop(0, n_chunks, fetch, 0)

    x_copy(x_hbm, b, j, slot).wait()

    wraps = s + 1 == n_steps
    ns = jnp.where(wraps, 0, s + 1)
    nl = jnp.where(wraps, l + 1, l)

    @pl.when(jnp.logical_and(jnp.logical_not(last), nl == 0))
    def _():
        x_copy(x_hbm, ns // n_tiles, ns % n_tiles, 1 - slot).start()

    @pl.when(jnp.logical_and(jnp.logical_not(last), nl > 0))
    def _():
        x_copy(o_hbm, ns // n_tiles, ns % n_tiles, 1 - slot).start()

    more_layers = l + 1 < depth

    @pl.when(jnp.logical_and(more_layers, s < n_chunks))
    def _():
        c_in, c_out = w_copies(l + 1, s, s % 2)
        c_in.start()
        c_out.start()

    @pl.when(jnp.logical_and(more_layers, s == 0))
    def _():
        make_pool_maps(l + 1, 1 - wslot)

    @pl.when(jnp.logical_and(more_layers, jnp.logical_and(s >= 1, s <= n_chunks)))
    def _():
        c_in, c_out = w_copies(l + 1, s - 1, (s - 1) % 2)
        c_in.wait()
        c_out.wait()
        cast_chunk(s - 1, (s - 1) % 2, 1 - wslot)

    @pl.when(step >= 2)
    def _():
        o_copy(b, j, slot).wait()

    @pl.when(j == 0)
    def _():
        zc_ref[...] = jnp.zeros((HALO, cw), jnp.float32)
        pc_ref[...] = jnp.zeros((HALO, pw), jnp.float32)

    x_ref = xbuf.at[slot]
    o_ref = obuf.at[slot]
    wi_ref = win_ref.at[wslot]
    wo_ref = wout_ref.at[wslot]
    wp_ref = wpair_ref.at[wslot]
    wc_ref = wconv_ref.at[l]

    mod = _pick_row(mod_ref, b)
    shift = mod[:, 0:dm]
    gain = _pick_row(gpre_ref, l) * (1.0 + mod[:, dm:2 * dm])
    out_gain = mod[:, 2 * dm:3 * dm] * _pick_row(gpost_ref, l)

    def norm(r0):
        x = x_ref[r0:r0 + sub, :]
        r = lax.rsqrt(jnp.mean(x * x, axis=-1, keepdims=True) + NORM_EPS)
        h_ref[r0:r0 + sub, :] = ((x * r) * gain + shift).astype(jnp.bfloat16)

    def project(r0, k):
        return jnp.dot(h_ref[r0:r0 + sub, :], wi_ref[:, k * blk:(k + 1) * blk],
                       preferred_element_type=jnp.float32)

    def lagged(e, lag):
        return pltpu.roll(e, lag, axis=0)[HALO:, :]

    def mix_conv(r0, d, res):
        cols = slice(d * gd, (d + 1) * gd)
        u, c, b_a, gate = (res[:, k * gd:(k + 1) * gd] for k in range(4))
        z = c * u
        ez = jnp.concatenate([zc_ref[:, cols], z], axis=0)
        zc_ref[:, cols] = z[sub - HALO:, :]
        conv = wc_ref[CONV_K - 1:CONV_K, cols] * z
        for k in range(CONV_K - 1):
            conv = conv + wc_ref[k:k + 1, cols] * lagged(ez, CONV_K - 1 - k)
        y_ref[r0:r0 + sub, cols] = (b_a * conv * _silu(gate)).astype(jnp.bfloat16)

    def mix_pool(r0, pair, res):
        pooled = []
        for k in range(2):
            g = 2 * pair + k
            w = POOL_WINDOWS[g]
            cols = slice(g * gd, (g + 1) * gd)
            v = res[:, k * gd:(k + 1) * gd]
            acc = jnp.concatenate([pc_ref[:, cols], v], axis=0)
            pc_ref[:, cols] = v[sub - HALO:, :]
            span = 1
            while span < min(w, 8):
                acc = acc + pltpu.roll(acc, span, axis=0)
                span *= 2
            if span < w:
                acc = acc[8:, :] + acc[:-8, :]
                acc = acc[HALO - 8:, :]
            else:
                acc = acc[HALO:, :]
            if r0 == 0:
                pos = (lax.broadcasted_iota(jnp.int32, (HALO, gd), 0) + (j * tt + 1)).astype(jnp.float32)
                inv = jnp.concatenate([1.0 / jnp.minimum(pos, float(w)),
                                       jnp.full((sub - HALO, gd), 1.0 / w, jnp.float32)], axis=0)
            else:
                inv = 1.0 / w
            pooled.append(acc * inv - v)
        mapped = jnp.dot(jnp.concatenate(pooled, axis=-1).astype(jnp.bfloat16), wp_ref[pair],
                         preferred_element_type=jnp.float32)
        return mapped, res[:, 2 * gd:4 * gd]

    def gate_pool(r0, pair, mapped, gate):
        y_ref[r0:r0 + sub, cw + pair * 2 * gd:cw + (pair + 1) * 2 * gd] = (
            mapped * _silu(gate)).astype(jnp.bfloat16)

    def finish(r0):
        rows = slice(r0, r0 + sub)
        yo = jnp.dot(y_ref[rows, :], wo_ref[...], preferred_element_type=jnp.float32)
        r2 = lax.rsqrt(jnp.mean(yo * yo, axis=-1, keepdims=True) + NORM_EPS)
        o_ref[rows, :] = x_ref[rows, :] + (yo * r2) * out_gain

    assert n_conv >= 2 and n_pair == 2
    norm(0)
    head = [project(0, 0), project(0, 1)]
    for r0 in range(0, tt, sub):
        nxt = r0 + sub if r0 + sub < tt else None
        if nxt is not None:
            norm(nxt)
        res = head
        for d in range(n_conv):
            if d + 2 < n_conv + n_pair:
                res.append(project(r0, d + 2))
            mix_conv(r0, d, res[d])
        head = []
        mapped = []
        for p in range(n_pair):
            mapped.append(mix_pool(r0, p, res[n_conv + p]))
            if nxt is not None:
                head.append(project(nxt, p))
        for p in range(n_pair):
            gate_pool(r0, p, *mapped[p])
        finish(r0)

    o_copy(b, j, slot).start()

    @pl.when(last)
    def _():
        o_copy(b, j, 1 - slot).wait()
        o_copy(b, j, slot).wait()


def _trunk(x, mod, g_pre, g_post, w_in, w_conv, w_pool, pool_scale, w_out):
    batch, seq, d = x.shape
    depth, _, in_cols = w_in.shape
    cw = w_conv.shape[2]
    _, groups, gd, _ = w_pool.shape
    pw = groups * gd
    tt = SEQ_TILE
    n_tiles = seq // tt
    n_steps = batch * n_tiles
    chunk = WEIGHT_CHUNK_ROWS
    assert w_in.shape[1] == w_out.shape[1] == d and d % chunk == 0
    assert cw == pw and cw % gd == 0 and groups % 2 == 0
    assert n_steps % 2 == 0 and d // chunk < n_steps and seq % tt == 0 and tt % SUB_ROWS == 0
    whole2 = lambda l, s: (0, 0)
    hbm = pl.BlockSpec(memory_space=pl.ANY)
    return pl.pallas_call(
        functools.partial(_trunk_kernel, cw=cw, gd=gd, sub=SUB_ROWS, n_tiles=n_tiles),
        grid=(depth, n_steps),
        in_specs=[
            hbm,
            pl.BlockSpec((None, batch, 3 * d), lambda l, s: (l, 0, 0)),
            pl.BlockSpec((depth, d), whole2),
            pl.BlockSpec((depth, d), whole2),
            hbm,
            pl.BlockSpec(w_conv.shape, lambda l, s: (0, 0, 0)),
            pl.BlockSpec(w_pool.shape, lambda l, s: (0, 0, 0, 0)),
            pl.BlockSpec((depth, pw), whole2),
            hbm,
        ],
        out_specs=hbm,
        out_shape=jax.ShapeDtypeStruct(x.shape, x.dtype),
        scratch_shapes=[
            pltpu.VMEM((2, tt, d), jnp.float32),
            pltpu.VMEM((2, tt, d), jnp.float32),
            pltpu.VMEM((2, d, in_cols), jnp.bfloat16),
            pltpu.VMEM((2, d, w_out.shape[2]), jnp.bfloat16),
            pltpu.VMEM((2, groups // 2, 2 * gd, 2 * gd), jnp.bfloat16),
            pltpu.VMEM((2, chunk, in_cols), jnp.float32),
            pltpu.VMEM((2, chunk, w_out.shape[2]), jnp.float32),
            pltpu.VMEM((HALO, cw), jnp.float32),
            pltpu.VMEM((HALO, pw), jnp.float32),
            pltpu.VMEM((tt, d), jnp.bfloat16),
            pltpu.VMEM((tt, cw + pw), jnp.bfloat16),
            pltpu.SemaphoreType.DMA((2,)),
            pltpu.SemaphoreType.DMA((2,)),
            pltpu.SemaphoreType.DMA((2, 2)),
        ],
        compiler_params=pltpu.CompilerParams(
            dimension_semantics=("arbitrary", "arbitrary"),
            vmem_limit_bytes=VMEM_LIMIT_BYTES),
        name="mixer_trunk",
    )(x, mod, g_pre, g_post, w_in, w_conv, w_pool, pool_scale, w_out)


def kernel(x, c, w_ada, b_ada, g_pre, w_in, w_conv, w_pool, pool_scale, w_out, g_post):
    mod = _ada_modulation(c, w_ada, b_ada)
    return _trunk(x, mod, g_pre, g_post, w_in, w_conv, w_pool, pool_scale, w_out)
```

```python
import functools

import jax
import jax.numpy as jnp
from jax import lax
from jax.experimental import pallas as pl
from jax.experimental.pallas import tpu as pltpu

CONV_K = 3
POOL_WINDOWS = (2, 4, 8, 16)
NORM_EPS = 1e-6
HALO = 16
SEQ_TILE = 1024
SUB_ROWS = 256
WEIGHT_CHUNK_ROWS = 128
VMEM_LIMIT_BYTES = 56 * 1024 * 1024


def _silu(v):
    return v * jax.nn.sigmoid(v)


def _pick_row(ref, idx):
    a = ref[...]
    rows = lax.broadcasted_iota(jnp.int32, a.shape, 0)
    return jnp.sum(jnp.where(rows == idx, a, 0.0), axis=0, keepdims=True)


def _ada_kernel(c_ref, w_ref, b_ref, o_ref):
    l = pl.program_id(0)
    c_act = _silu(c_ref[...]).astype(jnp.bfloat16)
    w = w_ref[...].astype(jnp.bfloat16)
    o_ref[...] = jnp.dot(c_act, w, preferred_element_type=jnp.float32) + b_ref[pl.ds(l, 1), :]


def _ada_modulation(c, w_ada, b_ada):
    depth, d, cols = w_ada.shape
    batch = c.shape[0]
    return pl.pallas_call(
        _ada_kernel,
        grid=(depth,),
        in_specs=[
            pl.BlockSpec((batch, d), lambda l: (0, 0)),
            pl.BlockSpec((None, d, cols), lambda l: (l, 0, 0)),
            pl.BlockSpec((depth, cols), lambda l: (0, 0)),
        ],
        out_specs=pl.BlockSpec((None, batch, cols), lambda l: (l, 0, 0)),
        out_shape=jax.ShapeDtypeStruct((depth, batch, cols), jnp.float32),
        compiler_params=pltpu.CompilerParams(
            dimension_semantics=("arbitrary",),
            vmem_limit_bytes=VMEM_LIMIT_BYTES),
        name="ada_modulation",
    )(c, w_ada, b_ada)


def _proj_block_order(cw, pw, gd):
    nb = cw // gd
    order = []
    for d in range(nb):
        order += [0 * nb + d, 2 * nb + d, 1 * nb + d, 3 * nb + d]
    for g in range(0, pw // gd, 2):
        order += [4 * nb + g, 4 * nb + g + 1, 4 * nb + pw // gd + g, 4 * nb + pw // gd + g + 1]
    return order


def _trunk_kernel(x_hbm, mod_ref, gpre_ref, gpost_ref, win_hbm, wconv_ref, wpool_ref, pscale_ref,
                  wout_hbm, o_hbm, xbuf, obuf, win_ref, wout_ref, wpair_ref, stage_in, stage_out,
                  zc_ref, pc_ref, h_ref, y_ref, sem_x, sem_o, sem_w, *, cw, gd, sub, n_tiles):
    l = pl.program_id(0)
    s = pl.program_id(1)
    depth = pl.num_programs(0)
    n_steps = pl.num_programs(1)
    _, tt, dm = xbuf.shape
    n_pool = len(POOL_WINDOWS)
    pw = gd * n_pool
    n_conv = cw // gd
    n_pair = n_pool // 2
    blk = 4 * gd
    order = _proj_block_order(cw, pw, gd)
    chunk = WEIGHT_CHUNK_ROWS
    n_chunks = win_hbm.shape[1] // chunk

    b = s // n_tiles
    j = s % n_tiles
    slot = s % 2
    wslot = l % 2
    step = l * n_steps + s
    first = step == 0
    last = step == depth * n_steps - 1

    def x_copy(src_hbm, bb, jj, sl):
        return pltpu.make_async_copy(src_hbm.at[bb, pl.ds(jj * tt, tt), :], xbuf.at[sl], sem_x.at[sl])

    def o_copy(bb, jj, sl):
        return pltpu.make_async_copy(obuf.at[sl], o_hbm.at[bb, pl.ds(jj * tt, tt), :], sem_o.at[sl])

    def w_copies(layer, k, st):
        rows = pl.ds(pl.multiple_of(k * chunk, chunk), chunk)
        return (pltpu.make_async_copy(win_hbm.at[layer, rows, :], stage_in.at[st], sem_w.at[0, st]),
                pltpu.make_async_copy(wout_hbm.at[layer, rows, :], stage_out.at[st], sem_w.at[1, st]))

    def cast_chunk(k, st, ws):
        rows = pl.ds(pl.multiple_of(k * chunk, chunk), chunk)
        for q, p in enumerate(order):
            win_ref[ws, rows, q * gd:(q + 1) * gd] = stage_in[st, :, p * gd:(p + 1) * gd].astype(jnp.bfloat16)
        for pair in range(n_pair):
            dst = slice((n_conv + pair) * blk, (n_conv + pair) * blk + 2 * gd)
            win_ref[ws, rows, dst] = jnp.dot(win_ref[ws, rows, dst], wpair_ref[ws, pair],
                                             preferred_element_type=jnp.float32).astype(jnp.bfloat16)
        wout_ref[ws, rows, :] = stage_out[st].astype(jnp.bfloat16)

    def make_pool_maps(layer, ws):
        zero = jnp.zeros((gd, gd), jnp.float32)
        scale = _pick_row(pscale_ref, layer)
        for g in range(0, n_pool, 2):
            scaled = [wpool_ref[layer, g + k] * scale[:, (g + k) * gd:(g + k + 1) * gd] for k in range(2)]
            wpair_ref[ws, g // 2] = jnp.concatenate(
                [jnp.concatenate([scaled[0], zero], axis=-1),
                 jnp.concatenate([zero, scaled[1]], axis=-1)], axis=0).astype(jnp.bfloat16)

    @pl.when(first)
    def _():
        x_copy(x_hbm, 0, 0, 0).start()
        make_pool_maps(0, 0)

        def fetch(k, carry):
            c_in, c_out = w_copies(0, k, 0)
            c_in.start()
            c_out.start()
            c_in.wait()
            c_out.wait()
            cast_chunk(k, 0, 0)
            return carry
        lax.fori_loop(0, n_chunks, fetch, 0)

    x_copy(x_hbm, b, j, slot).wait()

    wraps = s + 1 == n_steps
    ns = jnp.where(wraps, 0, s + 1)
    nl = jnp.where(wraps, l + 1, l)

    @pl.when(jnp.logical_and(jnp.logical_not(last), nl == 0))
    def _():
        x_copy(x_hbm, ns // n_tiles, ns % n_tiles, 1 - slot).start()

    @pl.when(jnp.logical_and(jnp.logical_not(last), nl > 0))
    def _():
        x_copy(o_hbm, ns // n_tiles, ns % n_tiles, 1 - slot).start()

    more_layers = l + 1 < depth

    @pl.when(jnp.logical_and(more_layers, s < n_chunks))
    def _():
        c_in, c_out = w_copies(l + 1, s, s % 2)
        c_in.start()
        c_out.start()

    @pl.when(jnp.logical_and(more_layers, s == 0))
    def _():
        make_pool_maps(l + 1, 1 - wslot)

    @pl.when(jnp.logical_and(more_layers, jnp.logical_and(s >= 1, s <= n_chunks)))
    def _():
        c_in, c_out = w_copies(l + 1, s - 1, (s - 1) % 2)
        c_in.wait()
        c_out.wait()
        cast_chunk(s - 1, (s - 1) % 2, 1 - wslot)

    @pl.when(step >= 2)
    def _():
        o_copy(b, j, slot).wait()

    @pl.when(j == 0)
    def _():
        zc_ref[...] = jnp.zeros((HALO, cw), jnp.float32)
        pc_ref[...] = jnp.zeros((HALO, pw), jnp.float32)

    x_ref = xbuf.at[slot]
    o_ref = obuf.at[slot]
    wi_ref = win_ref.at[wslot]
    wo_ref = wout_ref.at[wslot]
    wc_ref = wconv_ref.at[l]

    mod = _pick_row(mod_ref, b)
    shift = mod[:, 0:dm]
    gain = _pick_row(gpre_ref, l) * (1.0 + mod[:, dm:2 * dm])
    out_gain = mod[:, 2 * dm:3 * dm] * _pick_row(gpost_ref, l)

    def norm(r0):
        x = x_ref[r0:r0 + sub, :]
        r = lax.rsqrt(jnp.mean(x * x, axis=-1, keepdims=True) + NORM_EPS)
        h_ref[r0:r0 + sub, :] = ((x * r) * gain + shift).astype(jnp.bfloat16)

    def project(r0, k):
        return jnp.dot(h_ref[r0:r0 + sub, :], wi_ref[:, k * blk:(k + 1) * blk],
                       preferred_element_type=jnp.float32)

    def lagged(e, lag):
        return pltpu.roll(e, lag, axis=0)[HALO:, :]

    def mix_conv(r0, d, res):
        cols = slice(d * gd, (d + 1) * gd)
        u, c, b_a, gate = (res[:, k * gd:(k + 1) * gd] for k in range(4))
        z = c * u
        ez = jnp.concatenate([zc_ref[:, cols], z], axis=0)
        zc_ref[:, cols] = z[sub - HALO:, :]
        conv = wc_ref[CONV_K - 1:CONV_K, cols] * z
        for k in range(CONV_K - 1):
            conv = conv + wc_ref[k:k + 1, cols] * lagged(ez, CONV_K - 1 - k)
        y_ref[r0:r0 + sub, cols] = (b_a * conv * _silu(gate)).astype(jnp.bfloat16)

    def mix_pool(r0, pair, res):
        for k in range(2):
            g = 2 * pair + k
            w = POOL_WINDOWS[g]
            cols = slice(g * gd, (g + 1) * gd)
            v = res[:, k * gd:(k + 1) * gd]
            gate = res[:, (2 + k) * gd:(3 + k) * gd]
            acc = jnp.concatenate([pc_ref[:, cols], v], axis=0)
            pc_ref[:, cols] = v[sub - HALO:, :]
            span = 1
            while span < min(w, 8):
                acc = acc + pltpu.roll(acc, span, axis=0)
                span *= 2
            if span < w:
                acc = acc[8:, :] + acc[:-8, :]
                acc = acc[HALO - 8:, :]
            else:
                acc = acc[HALO:, :]
            if r0 == 0:
                pos = (lax.broadcasted_iota(jnp.int32, (HALO, gd), 0) + (j * tt + 1)).astype(jnp.float32)
                inv = jnp.concatenate([1.0 / jnp.minimum(pos, float(w)),
                                       jnp.full((sub - HALO, gd), 1.0 / w, jnp.float32)], axis=0)
            else:
                inv = 1.0 / w
            y_ref[r0:r0 + sub, cw + g * gd:cw + (g + 1) * gd] = (
                (acc * inv - v) * _silu(gate)).astype(jnp.bfloat16)

    def finish(r0):
        rows = slice(r0, r0 + sub)
        yo = jnp.dot(y_ref[rows, :], wo_ref[...], preferred_element_type=jnp.float32)
        r2 = lax.rsqrt(jnp.mean(yo * yo, axis=-1, keepdims=True) + NORM_EPS)
        o_ref[rows, :] = x_ref[rows, :] + (yo * r2) * out_gain

    assert n_conv >= 2 and n_pair == 2
    norm(0)
    head = [project(0, 0), project(0, 1)]
    for r0 in range(0, tt, sub):
        nxt = r0 + sub if r0 + sub < tt else None
        if nxt is not None:
            norm(nxt)
        res = head
        for d in range(n_conv):
            if d + 2 < n_conv + n_pair:
                res.append(project(r0, d + 2))
            mix_conv(r0, d, res[d])
        head = []
        for p in range(n_pair):
            mix_pool(r0, p, res[n_conv + p])
            if nxt is not None:
                head.append(project(nxt, p))
        finish(r0)

    o_copy(b, j, slot).start()

    @pl.when(last)
    def _():
        o_copy(b, j, 1 - slot).wait()
        o_copy(b, j, slot).wait()


def _trunk(x, mod, g_pre, g_post, w_in, w_conv, w_pool, pool_scale, w_out):
    batch, seq, d = x.shape
    depth, _, in_cols = w_in.shape
    cw = w_conv.shape[2]
    _, groups, gd, _ = w_pool.shape
    pw = groups * gd
    tt = SEQ_TILE
    n_tiles = seq // tt
    n_steps = batch * n_tiles
    chunk = WEIGHT_CHUNK_ROWS
    assert w_in.shape[1] == w_out.shape[1] == d and d % chunk == 0
    assert cw == pw and cw % gd == 0 and groups % 2 == 0
    assert n_steps % 2 == 0 and d // chunk < n_steps and seq % tt == 0 and tt % SUB_ROWS == 0
    whole2 = lambda l, s: (0, 0)
    hbm = pl.BlockSpec(memory_space=pl.ANY)
    return pl.pallas_call(
        functools.partial(_trunk_kernel, cw=cw, gd=gd, sub=SUB_ROWS, n_tiles=n_tiles),
        grid=(depth, n_steps),
        in_specs=[
            hbm,
            pl.BlockSpec((None, batch, 3 * d), lambda l, s: (l, 0, 0)),
            pl.BlockSpec((depth, d), whole2),
            pl.BlockSpec((depth, d), whole2),
            hbm,
            pl.BlockSpec(w_conv.shape, lambda l, s: (0, 0, 0)),
            pl.BlockSpec(w_pool.shape, lambda l, s: (0, 0, 0, 0)),
            pl.BlockSpec((depth, pw), whole2),
            hbm,
        ],
        out_specs=hbm,
        out_shape=jax.ShapeDtypeStruct(x.shape, x.dtype),
        scratch_shapes=[
            pltpu.VMEM((2, tt, d), jnp.float32),
            pltpu.VMEM((2, tt, d), jnp.float32),
            pltpu.VMEM((2, d, in_cols), jnp.bfloat16),
            pltpu.VMEM((2, d, w_out.shape[2]), jnp.bfloat16),
            pltpu.VMEM((2, groups // 2, 2 * gd, 2 * gd), jnp.bfloat16),
            pltpu.VMEM((2, chunk, in_cols), jnp.float32),
            pltpu.VMEM((2, chunk, w_out.shape[2]), jnp.float32),
            pltpu.VMEM((HALO, cw), jnp.float32),
            pltpu.VMEM((HALO, pw), jnp.float32),
            pltpu.VMEM((tt, d), jnp.bfloat16),
            pltpu.VMEM((tt, cw + pw), jnp.bfloat16),
            pltpu.SemaphoreType.DMA((2,)),
            pltpu.SemaphoreType.DMA((2,)),
            pltpu.SemaphoreType.DMA((2, 2)),
        ],
        compiler_params=pltpu.CompilerParams(
            dimension_semantics=("arbitrary", "arbitrary"),
            vmem_limit_bytes=VMEM_LIMIT_BYTES),
        name="mixer_trunk",
    )(x, mod, g_pre, g_post, w_in, w_conv, w_pool, pool_scale, w_out)


def kernel(x, c, w_ada, b_ada, g_pre, w_in, w_conv, w_pool, pool_scale, w_out, g_post):
    mod = _ada_modulation(c, w_ada, b_ada)
    return _trunk(x, mod, g_pre, g_post, w_in, w_conv, w_pool, pool_scale, w_out)
```

```python
import functools

import jax
import jax.numpy as jnp
from jax import lax
from jax.experimental import pallas as pl
from jax.experimental.pallas import tpu as pltpu

CONV_K = 3
POOL_WINDOWS = (2, 4, 8, 16)
NORM_EPS = 1e-6
HALO = 16
SUB_ROWS = 256
WEIGHT_CHUNK_ROWS = 128
RING = 3
VMEM_LIMIT_BYTES = 56 * 1024 * 1024


def _silu(v):
    return v * jax.nn.sigmoid(v)


def _pick_row(ref, idx):
    a = ref[...]
    rows = lax.broadcasted_iota(jnp.int32, a.shape, 0)
    return jnp.sum(jnp.where(rows == idx, a, 0.0), axis=0, keepdims=True)


def _ada_kernel(c_ref, w_ref, b_ref, o_ref):
    l = pl.program_id(0)
    c_act = _silu(c_ref[...]).astype(jnp.bfloat16)
    w = w_ref[...].astype(jnp.bfloat16)
    o_ref[...] = jnp.dot(c_act, w, preferred_element_type=jnp.float32) + b_ref[pl.ds(l, 1), :]


def _ada_modulation(c, w_ada, b_ada):
    depth, d, cols = w_ada.shape
    batch = c.shape[0]
    return pl.pallas_call(
        _ada_kernel,
        grid=(depth,),
        in_specs=[
            pl.BlockSpec((batch, d), lambda l: (0, 0)),
            pl.BlockSpec((None, d, cols), lambda l: (l, 0, 0)),
            pl.BlockSpec((depth, cols), lambda l: (0, 0)),
        ],
        out_specs=pl.BlockSpec((None, batch, cols), lambda l: (l, 0, 0)),
        out_shape=jax.ShapeDtypeStruct((depth, batch, cols), jnp.float32),
        compiler_params=pltpu.CompilerParams(
            dimension_semantics=("arbitrary",),
            vmem_limit_bytes=VMEM_LIMIT_BYTES),
        name="ada_modulation",
    )(c, w_ada, b_ada)


def _proj_block_order(cw, pw, gd):
    nb = cw // gd
    order = []
    for d in range(nb):
        order += [0 * nb + d, 2 * nb + d, 1 * nb + d, 3 * nb + d]
    for g in range(0, pw // gd, 2):
        order += [4 * nb + g, 4 * nb + g + 1, 4 * nb + pw // gd + g, 4 * nb + pw // gd + g + 1]
    return order


def _trunk_kernel(x_hbm, mod_ref, gpre_ref, gpost_ref, win_hbm, wconv_ref, wpool_ref, pscale_ref,
                  wout_hbm, o_hbm, xbuf, win_ref, wout_ref, wpair_ref, stage_in, stage_out,
                  zc_ref, pc_ref, h_ref, y_ref, sem_x, sem_o, sem_w, *, cw, gd, sub):
    l = pl.program_id(0)
    b = pl.program_id(1)
    depth = pl.num_programs(0)
    n_steps = pl.num_programs(1)
    _, tt, dm = xbuf.shape
    n_pool = len(POOL_WINDOWS)
    pw = gd * n_pool
    n_conv = cw // gd
    n_pair = n_pool // 2
    blk = 4 * gd
    order = _proj_block_order(cw, pw, gd)
    chunk = WEIGHT_CHUNK_ROWS
    n_chunks = win_hbm.shape[1] // chunk

    wslot = l % 2
    step = l * n_steps + b
    slot = step % RING
    next_slot = (step + 1) % RING
    prev_slot = (step + RING - 1) % RING
    first = step == 0
    last = step == depth * n_steps - 1

    def x_copy(src_hbm, bb, sl):
        return pltpu.make_async_copy(src_hbm.at[bb], xbuf.at[sl], sem_x.at[sl])

    def o_copy(bb, sl):
        return pltpu.make_async_copy(xbuf.at[sl], o_hbm.at[bb], sem_o.at[sl])

    def w_copies(layer, k, st):
        rows = pl.ds(pl.multiple_of(k * chunk, chunk), chunk)
        return (pltpu.make_async_copy(win_hbm.at[layer, rows, :], stage_in.at[st], sem_w.at[0, st]),
                pltpu.make_async_copy(wout_hbm.at[layer, rows, :], stage_out.at[st], sem_w.at[1, st]))

    def cast_chunk(k, st, ws):
        rows = pl.ds(pl.multiple_of(k * chunk, chunk), chunk)
        for q, p in enumerate(order):
            win_ref[ws, rows, q * gd:(q + 1) * gd] = stage_in[st, :, p * gd:(p + 1) * gd].astype(jnp.bfloat16)
        wout_ref[ws, rows, :] = stage_out[st].astype(jnp.bfloat16)

    def make_pool_maps(layer, ws):
        zero = jnp.zeros((gd, gd), jnp.float32)
        scale = _pick_row(pscale_ref, layer)
        for g in range(0, n_pool, 2):
            scaled = [wpool_ref[layer, g + k] * scale[:, (g + k) * gd:(g + k + 1) * gd] for k in range(2)]
            wpair_ref[ws, g // 2] = jnp.concatenate(
                [jnp.concatenate([scaled[0], zero], axis=-1),
                 jnp.concatenate([zero, scaled[1]], axis=-1)], axis=0).astype(jnp.bfloat16)

    @pl.when(first)
    def _():
        x_copy(x_hbm, 0, 0).start()
        make_pool_maps(0, 0)

        def fetch(k, carry):
            c_in, c_out = w_copies(0, k, 0)
            c_in.start()
            c_out.start()
            c_in.wait()
            c_out.wait()
            cast_chunk(k, 0, 0)
            return carry
        lax.fori_loop(0, n_chunks, fetch, 0)

    x_copy(x_hbm, b, slot).wait()

    @pl.when(step >= 2)
    def _():
        o_copy(b, next_slot).wait()

    wraps = b + 1 == n_steps
    nb = jnp.where(wraps, 0, b + 1)
    nl = jnp.where(wraps, l + 1, l)

    @pl.when(jnp.logical_and(jnp.logical_not(last), nl == 0))
    def _():
        x_copy(x_hbm, nb, next_slot).start()

    @pl.when(jnp.logical_and(jnp.logical_not(last), nl > 0))
    def _():
        x_copy(o_hbm, nb, next_slot).start()

    more_layers = l + 1 < depth

    @pl.when(jnp.logical_and(more_layers, b < n_chunks))
    def _():
        c_in, c_out = w_copies(l + 1, b, b % 2)
        c_in.start()
        c_out.start()

    @pl.when(jnp.logical_and(more_layers, b == 0))
    def _():
        make_pool_maps(l + 1, 1 - wslot)

    @pl.when(jnp.logical_and(more_layers, b >= 1))
    def _():
        c_in, c_out = w_copies(l + 1, b - 1, (b - 1) % 2)
        c_in.wait()
        c_out.wait()
        cast_chunk(b - 1, (b - 1) % 2, 1 - wslot)

    @pl.when(jnp.logical_and(l >= 1, b == 0))
    def _():
        c_in, c_out = w_copies(l, n_chunks - 1, (n_chunks - 1) % 2)
        c_in.wait()
        c_out.wait()
        cast_chunk(n_chunks - 1, (n_chunks - 1) % 2, wslot)

    zc_ref[...] = jnp.zeros((HALO, cw), jnp.float32)
    pc_ref[...] = jnp.zeros((HALO, pw), jnp.float32)

    x_ref = xbuf.at[slot]
    wi_ref = win_ref.at[wslot]
    wo_ref = wout_ref.at[wslot]
    wp_ref = wpair_ref.at[wslot]
    wc_ref = wconv_ref.at[l]

    mod = _pick_row(mod_ref, b)
    shift = mod[:, 0:dm]
    gain = _pick_row(gpre_ref, l) * (1.0 + mod[:, dm:2 * dm])
    out_gain = mod[:, 2 * dm:3 * dm] * _pick_row(gpost_ref, l)

    def norm(i):
        x = x_ref[i * sub:(i + 1) * sub, :]
        r = lax.rsqrt(jnp.mean(x * x, axis=-1, keepdims=True) + NORM_EPS)
        h_ref[i % 2] = ((x * r) * gain + shift).astype(jnp.bfloat16)

    def project(i, k):
        return jnp.dot(h_ref[i % 2], wi_ref[:, k * blk:(k + 1) * blk], preferred_element_type=jnp.float32)

    def lagged(e, lag):
        return pltpu.roll(e, lag, axis=0)[HALO:, :]

    def mix_conv(i, d, res):
        cols = slice(d * gd, (d + 1) * gd)
        u, c, b_a, gate = (res[:, k * gd:(k + 1) * gd] for k in range(4))
        z = c * u
        ez = jnp.concatenate([zc_ref[:, cols], z], axis=0)
        zc_ref[:, cols] = z[sub - HALO:, :]
        conv = wc_ref[CONV_K - 1:CONV_K, cols] * z
        for k in range(CONV_K - 1):
            conv = conv + wc_ref[k:k + 1, cols] * lagged(ez, CONV_K - 1 - k)
        y_ref[i % 2, :, cols] = (b_a * conv * _silu(gate)).astype(jnp.bfloat16)

    def mix_pool(i, pair, res):
        pooled = []
        for k in range(2):
            g = 2 * pair + k
            w = POOL_WINDOWS[g]
            cols = slice(g * gd, (g + 1) * gd)
            v = res[:, k * gd:(k + 1) * gd]
            acc = jnp.concatenate([pc_ref[:, cols], v], axis=0)
            pc_ref[:, cols] = v[sub - HALO:, :]
            span = 1
            while span < min(w, 8):
                acc = acc + pltpu.roll(acc, span, axis=0)
                span *= 2
            if span < w:
                acc = acc[8:, :] + acc[:-8, :]
                acc = acc[HALO - 8:, :]
            else:
                acc = acc[HALO:, :]
            if i == 0:
                pos = (lax.broadcasted_iota(jnp.int32, (HALO, gd), 0) + 1).astype(jnp.float32)
                inv = jnp.concatenate([1.0 / jnp.minimum(pos, float(w)),
                                       jnp.full((sub - HALO, gd), 1.0 / w, jnp.float32)], axis=0)
            else:
                inv = 1.0 / w
            pooled.append(acc * inv - v)
        mapped = jnp.dot(jnp.concatenate(pooled, axis=-1).astype(jnp.bfloat16), wp_ref[pair],
                         preferred_element_type=jnp.float32)
        return mapped, res[:, 2 * gd:4 * gd]

    def gate_pool(i, pair, mapped, gate):
        y_ref[i % 2, :, cw + pair * 2 * gd:cw + (pair + 1) * 2 * gd] = (
            mapped * _silu(gate)).astype(jnp.bfloat16)

    def finish(i):
        rows = slice(i * sub, (i + 1) * sub)
        yo = jnp.dot(y_ref[i % 2], wo_ref[...], preferred_element_type=jnp.float32)
        r2 = lax.rsqrt(jnp.mean(yo * yo, axis=-1, keepdims=True) + NORM_EPS)
        x_ref[rows, :] = x_ref[rows, :] + (yo * r2) * out_gain

    assert n_conv >= 2 and n_pair == 2
    n_chains = tt // sub
    norm(0)
    head = [project(0, 0), project(0, 1)]
    for i in range(n_chains):
        more = i + 1 < n_chains
        if more:
            norm(i + 1)
        res = head
        for d in range(n_conv):
            if d + 2 < n_conv + n_pair:
                res.append(project(i, d + 2))
            mix_conv(i, d, res[d])
        head = []
        mapped = []
        for p in range(n_pair):
            mapped.append(mix_pool(i, p, res[n_conv + p]))
            if more:
                head.append(project(i + 1, p))
        for p in range(n_pair):
            gate_pool(i, p, *mapped[p])
        finish(i)

    o_copy(b, slot).start()

    @pl.when(last)
    def _():
        o_copy(b, prev_slot).wait()
        o_copy(b, slot).wait()


def _trunk(x, mod, g_pre, g_post, w_in, w_conv, w_pool, pool_scale, w_out):
    batch, seq, d = x.shape
    depth, _, in_cols = w_in.shape
    cw = w_conv.shape[2]
    _, groups, gd, _ = w_pool.shape
    pw = groups * gd
    chunk = WEIGHT_CHUNK_ROWS
    assert w_in.shape[1] == w_out.shape[1] == d and d % chunk == 0
    assert cw == pw and cw % gd == 0 and groups % 2 == 0
    assert d // chunk == batch and seq % SUB_ROWS == 0 and batch * depth >= RING
    whole2 = lambda l, b: (0, 0)
    hbm = pl.BlockSpec(memory_space=pl.ANY)
    return pl.pallas_call(
        functools.partial(_trunk_kernel, cw=cw, gd=gd, sub=SUB_ROWS),
        grid=(depth, batch),
        in_specs=[
            hbm,
            pl.BlockSpec((None, batch, 3 * d), lambda l, b: (l, 0, 0)),
            pl.BlockSpec((depth, d), whole2),
            pl.BlockSpec((depth, d), whole2),
            hbm,
            pl.BlockSpec(w_conv.shape, lambda l, b: (0, 0, 0)),
            pl.BlockSpec(w_pool.shape, lambda l, b: (0, 0, 0, 0)),
            pl.BlockSpec((depth, pw), whole2),
            hbm,
        ],
        out_specs=hbm,
        out_shape=jax.ShapeDtypeStruct(x.shape, x.dtype),
        scratch_shapes=[
            pltpu.VMEM((RING, seq, d), jnp.float32),
            pltpu.VMEM((2, d, in_cols), jnp.bfloat16),
            pltpu.VMEM((2, d, w_out.shape[2]), jnp.bfloat16),
            pltpu.VMEM((2, groups // 2, 2 * gd, 2 * gd), jnp.bfloat16),
            pltpu.VMEM((2, chunk, in_cols), jnp.float32),
            pltpu.VMEM((2, chunk, w_out.shape[2]), jnp.float32),
            pltpu.VMEM((HALO, cw), jnp.float32),
            pltpu.VMEM((HALO, pw), jnp.float32),
            pltpu.VMEM((2, SUB_ROWS, d), jnp.bfloat16),
            pltpu.VMEM((2, SUB_ROWS, cw + pw), jnp.bfloat16),
            pltpu.SemaphoreType.DMA((RING,)),
            pltpu.SemaphoreType.DMA((RING,)),
            pltpu.SemaphoreType.DMA((2, 2)),
        ],
        compiler_params=pltpu.CompilerParams(
            dimension_semantics=("arbitrary", "arbitrary"),
            vmem_limit_bytes=VMEM_LIMIT_BYTES),
        name="mixer_trunk",
    )(x, mod, g_pre, g_post, w_in, w_conv, w_pool, pool_scale, w_out)


def kernel(x, c, w_ada, b_ada, g_pre, w_in, w_conv, w_pool, pool_scale, w_out, g_post):
    mod = _ada_modulation(c, w_ada, b_ada)
    return _trunk(x, mod, g_pre, g_post, w_in, w_conv, w_pool, pool_scale, w_out)
```

```python
import functools

import jax
import jax.numpy as jnp
from jax import lax
from jax.experimental import pallas as pl
from jax.experimental.pallas import tpu as pltpu

CONV_K = 3
POOL_WINDOWS = (2, 4, 8, 16)
NORM_EPS = 1e-6
HALO = 16
SUB_ROWS = 256
WEIGHT_CHUNK_ROWS = 128
RING = 3
VMEM_LIMIT_BYTES = 56 * 1024 * 1024


def _silu(v):
    return v * jax.nn.sigmoid(v)


def _pick_row(ref, idx):
    a = ref[...]
    rows = lax.broadcasted_iota(jnp.int32, a.shape, 0)
    return jnp.sum(jnp.where(rows == idx, a, 0.0), axis=0, keepdims=True)


def _ada_kernel(c_ref, w_ref, b_ref, o_ref):
    l = pl.program_id(0)
    c_act = _silu(c_ref[...]).astype(jnp.bfloat16)
    w = w_ref[...].astype(jnp.bfloat16)
    o_ref[...] = jnp.dot(c_act, w, preferred_element_type=jnp.float32) + b_ref[pl.ds(l, 1), :]


def _ada_modulation(c, w_ada, b_ada):
    depth, d, cols = w_ada.shape
    batch = c.shape[0]
    return pl.pallas_call(
        _ada_kernel,
        grid=(depth,),
        in_specs=[
            pl.BlockSpec((batch, d), lambda l: (0, 0)),
            pl.BlockSpec((None, d, cols), lambda l: (l, 0, 0)),
            pl.BlockSpec((depth, cols), lambda l: (0, 0)),
        ],
        out_specs=pl.BlockSpec((None, batch, cols), lambda l: (l, 0, 0)),
        out_shape=jax.ShapeDtypeStruct((depth, batch, cols), jnp.float32),
        compiler_params=pltpu.CompilerParams(
            dimension_semantics=("arbitrary",),
            vmem_limit_bytes=VMEM_LIMIT_BYTES),
        name="ada_modulation",
    )(c, w_ada, b_ada)


def _proj_block_order(cw, pw, gd):
    nb = cw // gd
    order = []
    for d in range(nb):
        order += [0 * nb + d, 2 * nb + d, 1 * nb + d, 3 * nb + d]
    for g in range(0, pw // gd, 2):
        order += [4 * nb + g, 4 * nb + g + 1, 4 * nb + pw // gd + g, 4 * nb + pw // gd + g + 1]
    return order


def _trunk_kernel(x_hbm, mod_ref, gpre_ref, gpost_ref, win_hbm, wconv_ref, wpool_ref, pscale_ref,
                  wout_hbm, o_hbm, xbuf, win_ref, wout_ref, wpair_ref, stage_in, stage_out,
                  zc_ref, pc_ref, h_ref, y_ref, sem_x, sem_o, sem_w, *, cw, gd, sub):
    l = pl.program_id(0)
    b = pl.program_id(1)
    depth = pl.num_programs(0)
    n_steps = pl.num_programs(1)
    _, tt, dm = xbuf.shape
    n_pool = len(POOL_WINDOWS)
    pw = gd * n_pool
    n_conv = cw // gd
    n_pair = n_pool // 2
    blk = 4 * gd
    order = _proj_block_order(cw, pw, gd)
    chunk = WEIGHT_CHUNK_ROWS
    n_chunks = win_hbm.shape[1] // chunk

    wslot = l % 2
    step = l * n_steps + b
    slot = step % RING
    next_slot = (step + 1) % RING
    prev_slot = (step + RING - 1) % RING
    first = step == 0
    last = step == depth * n_steps - 1

    def x_copy(src_hbm, bb, sl):
        return pltpu.make_async_copy(src_hbm.at[bb], xbuf.at[sl], sem_x.at[sl])

    def o_copy(bb, sl):
        return pltpu.make_async_copy(xbuf.at[sl], o_hbm.at[bb], sem_o.at[sl])

    def w_copies(layer, k, st):
        rows = pl.ds(pl.multiple_of(k * chunk, chunk), chunk)
        return (pltpu.make_async_copy(win_hbm.at[layer, rows, :], stage_in.at[st], sem_w.at[0, st]),
                pltpu.make_async_copy(wout_hbm.at[layer, rows, :], stage_out.at[st], sem_w.at[1, st]))

    def cast_chunk(k, st, ws):
        rows = pl.ds(pl.multiple_of(k * chunk, chunk), chunk)
        for q, p in enumerate(order):
            win_ref[ws, rows, q * gd:(q + 1) * gd] = stage_in[st, :, p * gd:(p + 1) * gd].astype(jnp.bfloat16)
        for pair in range(n_pair):
            dst = slice((n_conv + pair) * blk, (n_conv + pair) * blk + 2 * gd)
            win_ref[ws, rows, dst] = jnp.dot(win_ref[ws, rows, dst], wpair_ref[ws, pair],
                                             preferred_element_type=jnp.float32).astype(jnp.bfloat16)
        wout_ref[ws, rows, :] = stage_out[st].astype(jnp.bfloat16)

    def make_pool_maps(layer, ws):
        zero = jnp.zeros((gd, gd), jnp.float32)
        scale = _pick_row(pscale_ref, layer)
        for g in range(0, n_pool, 2):
            scaled = [wpool_ref[layer, g + k] * scale[:, (g + k) * gd:(g + k + 1) * gd] for k in range(2)]
            wpair_ref[ws, g // 2] = jnp.concatenate(
                [jnp.concatenate([scaled[0], zero], axis=-1),
                 jnp.concatenate([zero, scaled[1]], axis=-1)], axis=0).astype(jnp.bfloat16)

    @pl.when(first)
    def _():
        x_copy(x_hbm, 0, 0).start()
        make_pool_maps(0, 0)

        def fetch(k, carry):
            c_in, c_out = w_copies(0, k, 0)
            c_in.start()
            c_out.start()
            c_in.wait()
            c_out.wait()
            cast_chunk(k, 0, 0)
            return carry
        lax.fori_loop(0, n_chunks, fetch, 0)

    x_copy(x_hbm, b, slot).wait()

    @pl.when(step >= 2)
    def _():
        o_copy(b, next_slot).wait()

    wraps = b + 1 == n_steps
    nb = jnp.where(wraps, 0, b + 1)
    nl = jnp.where(wraps, l + 1, l)

    @pl.when(jnp.logical_and(jnp.logical_not(last), nl == 0))
    def _():
        x_copy(x_hbm, nb, next_slot).start()

    @pl.when(jnp.logical_and(jnp.logical_not(last), nl > 0))
    def _():
        x_copy(o_hbm, nb, next_slot).start()

    more_layers = l + 1 < depth

    @pl.when(jnp.logical_and(more_layers, b < n_chunks))
    def _():
        c_in, c_out = w_copies(l + 1, b, b % 2)
        c_in.start()
        c_out.start()

    @pl.when(jnp.logical_and(more_layers, b == 0))
    def _():
        make_pool_maps(l + 1, 1 - wslot)

    @pl.when(jnp.logical_and(more_layers, b >= 1))
    def _():
        c_in, c_out = w_copies(l + 1, b - 1, (b - 1) % 2)
        c_in.wait()
        c_out.wait()
        cast_chunk(b - 1, (b - 1) % 2, 1 - wslot)

    @pl.when(jnp.logical_and(l >= 1, b == 0))
    def _():
        c_in, c_out = w_copies(l, n_chunks - 1, (n_chunks - 1) % 2)
        c_in.wait()
        c_out.wait()
        cast_chunk(n_chunks - 1, (n_chunks - 1) % 2, wslot)

    zc_ref[...] = jnp.zeros((HALO, cw), jnp.float32)
    pc_ref[...] = jnp.zeros((HALO, pw), jnp.float32)

    x_ref = xbuf.at[slot]
    wi_ref = win_ref.at[wslot]
    wo_ref = wout_ref.at[wslot]
    wc_ref = wconv_ref.at[l]

    mod = _pick_row(mod_ref, b)
    shift = mod[:, 0:dm]
    gain = _pick_row(gpre_ref, l) * (1.0 + mod[:, dm:2 * dm])
    out_gain = mod[:, 2 * dm:3 * dm] * _pick_row(gpost_ref, l)

    def norm(i):
        x = x_ref[i * sub:(i + 1) * sub, :]
        r = lax.rsqrt(jnp.mean(x * x, axis=-1, keepdims=True) + NORM_EPS)
        h_ref[i % 2] = ((x * r) * gain + shift).astype(jnp.bfloat16)

    def project(i, k):
        return jnp.dot(h_ref[i % 2], wi_ref[:, k * blk:(k + 1) * blk], preferred_element_type=jnp.float32)

    def lagged(e, lag):
        return pltpu.roll(e, lag, axis=0)[HALO:, :]

    def mix_conv(i, d, res):
        cols = slice(d * gd, (d + 1) * gd)
        u, c, b_a, gate = (res[:, k * gd:(k + 1) * gd] for k in range(4))
        z = c * u
        ez = jnp.concatenate([zc_ref[:, cols], z], axis=0)
        zc_ref[:, cols] = z[sub - HALO:, :]
        conv = wc_ref[CONV_K - 1:CONV_K, cols] * z
        for k in range(CONV_K - 1):
            conv = conv + wc_ref[k:k + 1, cols] * lagged(ez, CONV_K - 1 - k)
        y_ref[i % 2, :, cols] = (b_a * conv * _silu(gate)).astype(jnp.bfloat16)

    def mix_pool(i, pair, res):
        for k in range(2):
            g = 2 * pair + k
            w = POOL_WINDOWS[g]
            cols = slice(g * gd, (g + 1) * gd)
            v = res[:, k * gd:(k + 1) * gd]
            gate = res[:, (2 + k) * gd:(3 + k) * gd]
            acc = jnp.concatenate([pc_ref[:, cols], v], axis=0)
            pc_ref[:, cols] = v[sub - HALO:, :]
            span = 1
            while span < min(w, 8):
                acc = acc + pltpu.roll(acc, span, axis=0)
                span *= 2
            if span < w:
                acc = acc[8:, :] + acc[:-8, :]
                acc = acc[HALO - 8:, :]
            else:
                acc = acc[HALO:, :]
            if i == 0:
                pos = (lax.broadcasted_iota(jnp.int32, (HALO, gd), 0) + 1).astype(jnp.float32)
                inv = jnp.concatenate([1.0 / jnp.minimum(pos, float(w)),
                                       jnp.full((sub - HALO, gd), 1.0 / w, jnp.float32)], axis=0)
            else:
                inv = 1.0 / w
            y_ref[i % 2, :, cw + g * gd:cw + (g + 1) * gd] = (
                (acc * inv - v) * _silu(gate)).astype(jnp.bfloat16)

    def finish(i):
        rows = slice(i * sub, (i + 1) * sub)
        yo = jnp.dot(y_ref[i % 2], wo_ref[...], preferred_element_type=jnp.float32)
        r2 = lax.rsqrt(jnp.mean(yo * yo, axis=-1, keepdims=True) + NORM_EPS)
        x_ref[rows, :] = x_ref[rows, :] + (yo * r2) * out_gain

    assert n_conv >= 2 and n_pair == 2
    n_chains = tt // sub
    norm(0)
    head = [project(0, 0), project(0, 1)]
    for i in range(n_chains):
        more = i + 1 < n_chains
        if more:
            norm(i + 1)
        res = head
        for d in range(n_conv):
            if d + 2 < n_conv + n_pair:
                res.append(project(i, d + 2))
            mix_conv(i, d, res[d])
        head = []
        for p in range(n_pair):
            mix_pool(i, p, res[n_conv + p])
            if more:
                head.append(project(i + 1, p))
        finish(i)

    o_copy(b, slot).start()

    @pl.when(last)
    def _():
        o_copy(b, prev_slot).wait()
        o_copy(b, slot).wait()


def _trunk(x, mod, g_pre, g_post, w_in, w_conv, w_pool, pool_scale, w_out):
    batch, seq, d = x.shape
    depth, _, in_cols = w_in.shape
    cw = w_conv.shape[2]
    _, groups, gd, _ = w_pool.shape
    pw = groups * gd
    chunk = WEIGHT_CHUNK_ROWS
    assert w_in.shape[1] == w_out.shape[1] == d and d % chunk == 0
    assert cw == pw and cw % gd == 0 and groups % 2 == 0
    assert d // chunk == batch and seq % SUB_ROWS == 0 and batch * depth >= RING
    whole2 = lambda l, b: (0, 0)
    hbm = pl.BlockSpec(memory_space=pl.ANY)
    return pl.pallas_call(
        functools.partial(_trunk_kernel, cw=cw, gd=gd, sub=SUB_ROWS),
        grid=(depth, batch),
        in_specs=[
            hbm,
            pl.BlockSpec((None, batch, 3 * d), lambda l, b: (l, 0, 0)),
            pl.BlockSpec((depth, d), whole2),
            pl.BlockSpec((depth, d), whole2),
            hbm,
            pl.BlockSpec(w_conv.shape, lambda l, b: (0, 0, 0)),
            pl.BlockSpec(w_pool.shape, lambda l, b: (0, 0, 0, 0)),
            pl.BlockSpec((depth, pw), whole2),
            hbm,
        ],
        out_specs=hbm,
        out_shape=jax.ShapeDtypeStruct(x.shape, x.dtype),
        scratch_shapes=[
            pltpu.VMEM((RING, seq, d), jnp.float32),
            pltpu.VMEM((2, d, in_cols), jnp.bfloat16),
            pltpu.VMEM((2, d, w_out.shape[2]), jnp.bfloat16),
            pltpu.VMEM((2, groups // 2, 2 * gd, 2 * gd), jnp.bfloat16),
            pltpu.VMEM((2, chunk, in_cols), jnp.float32),
            pltpu.VMEM((2, chunk, w_out.shape[2]), jnp.float32),
            pltpu.VMEM((HALO, cw), jnp.float32),
            pltpu.VMEM((HALO, pw), jnp.float32),
            pltpu.VMEM((2, SUB_ROWS, d), jnp.bfloat16),
            pltpu.VMEM((2, SUB_ROWS, cw + pw), jnp.bfloat16),
            pltpu.SemaphoreType.DMA((RING,)),
            pltpu.SemaphoreType.DMA((RING,)),
            pltpu.SemaphoreType.DMA((2, 2)),
        ],
        compiler_params=pltpu.CompilerParams(
            dimension_semantics=("arbitrary", "arbitrary"),
            vmem_limit_bytes=VMEM_LIMIT_BYTES),
        name="mixer_trunk",
    )(x, mod, g_pre, g_post, w_in, w_conv, w_pool, pool_scale, w_out)


def kernel(x, c, w_ada, b_ada, g_pre, w_in, w_conv, w_pool, pool_scale, w_out, g_post):
    mod = _ada_modulation(c, w_ada, b_ada)
    return _trunk(x, mod, g_pre, g_post, w_in, w_conv, w_pool, pool_scale, w_out)
```

```python
import functools

import jax
import jax.numpy as jnp
from jax import lax
from jax.experimental import pallas as pl
from jax.experimental.pallas import tpu as pltpu

CONV_K = 3
POOL_WINDOWS = (2, 4, 8, 16)
NORM_EPS = 1e-6
HALO = 16
SUB_ROWS = 256
WEIGHT_CHUNK_ROWS = 128
RING = 3
VMEM_LIMIT_BYTES = 56 * 1024 * 1024


def _silu(v):
    return v * jax.nn.sigmoid(v)


def _pick_row(ref, idx):
    a = ref[...]
    rows = lax.broadcasted_iota(jnp.int32, a.shape, 0)
    return jnp.sum(jnp.where(rows == idx, a, 0.0), axis=0, keepdims=True)


def _ada_kernel(c_ref, w_ref, b_ref, o_ref):
    l = pl.program_id(0)
    c_act = _silu(c_ref[...]).astype(jnp.bfloat16)
    w = w_ref[...].astype(jnp.bfloat16)
    o_ref[...] = jnp.dot(c_act, w, preferred_element_type=jnp.float32) + b_ref[pl.ds(l, 1), :]


def _ada_modulation(c, w_ada, b_ada):
    depth, d, cols = w_ada.shape
    batch = c.shape[0]
    return pl.pallas_call(
        _ada_kernel,
        grid=(depth,),
        in_specs=[
            pl.BlockSpec((batch, d), lambda l: (0, 0)),
            pl.BlockSpec((None, d, cols), lambda l: (l, 0, 0)),
            pl.BlockSpec((depth, cols), lambda l: (0, 0)),
        ],
        out_specs=pl.BlockSpec((None, batch, cols), lambda l: (l, 0, 0)),
        out_shape=jax.ShapeDtypeStruct((depth, batch, cols), jnp.float32),
        compiler_params=pltpu.CompilerParams(
            dimension_semantics=("arbitrary",),
            vmem_limit_bytes=VMEM_LIMIT_BYTES),
        name="ada_modulation",
    )(c, w_ada, b_ada)


def _proj_block_order(cw, pw, gd):
    nb = cw // gd
    order = []
    for d in range(nb):
        order += [0 * nb + d, 2 * nb + d, 1 * nb + d, 3 * nb + d]
    for g in range(0, pw // gd, 2):
        order += [4 * nb + g, 4 * nb + g + 1, 4 * nb + pw // gd + g, 4 * nb + pw // gd + g + 1]
    return order


def _trunk_kernel(x_hbm, mod_ref, gpre_ref, gpost_ref, win_hbm, wconv_ref, wpool_ref, pscale_ref,
                  wout_hbm, o_hbm, xbuf, win_ref, wout_ref, wpair_ref, stage_in, stage_out,
                  zc_ref, pc_ref, h_ref, y_ref, sem_x, sem_o, sem_w, *, cw, gd, sub):
    l = pl.program_id(0)
    b = pl.program_id(1)
    depth = pl.num_programs(0)
    n_steps = pl.num_programs(1)
    _, tt, dm = xbuf.shape
    n_pool = len(POOL_WINDOWS)
    pw = gd * n_pool
    n_conv = cw // gd
    n_pair = n_pool // 2
    blk = 4 * gd
    order = _proj_block_order(cw, pw, gd)
    chunk = WEIGHT_CHUNK_ROWS
    n_chunks = win_hbm.shape[1] // chunk

    wslot = l % 2
    step = l * n_steps + b
    slot = step % RING
    next_slot = (step + 1) % RING
    prev_slot = (step + RING - 1) % RING
    first = step == 0
    last = step == depth * n_steps - 1

    def x_copy(src_hbm, bb, sl):
        return pltpu.make_async_copy(src_hbm.at[bb], xbuf.at[sl], sem_x.at[sl])

    def o_copy(bb, sl):
        return pltpu.make_async_copy(xbuf.at[sl], o_hbm.at[bb], sem_o.at[sl])

    def w_copies(layer, k, st):
        rows = pl.ds(pl.multiple_of(k * chunk, chunk), chunk)
        return (pltpu.make_async_copy(win_hbm.at[layer, rows, :], stage_in.at[st], sem_w.at[0, st]),
                pltpu.make_async_copy(wout_hbm.at[layer, rows, :], stage_out.at[st], sem_w.at[1, st]))

    def cast_chunk(k, st, ws):
        rows = pl.ds(pl.multiple_of(k * chunk, chunk), chunk)
        for q, p in enumerate(order):
            win_ref[ws, rows, q * gd:(q + 1) * gd] = stage_in[st, :, p * gd:(p + 1) * gd].astype(jnp.bfloat16)
        wout_ref[ws, rows, :] = stage_out[st].astype(jnp.bfloat16)

    def make_pool_maps(layer, ws):
        zero = jnp.zeros((gd, gd), jnp.float32)
        scale = _pick_row(pscale_ref, layer)
        for g in range(0, n_pool, 2):
            scaled = [wpool_ref[layer, g + k] * scale[:, (g + k) * gd:(g + k + 1) * gd] for k in range(2)]
            wpair_ref[ws, g // 2] = jnp.concatenate(
                [jnp.concatenate([scaled[0], zero], axis=-1),
                 jnp.concatenate([zero, scaled[1]], axis=-1)], axis=0).astype(jnp.bfloat16)

    @pl.when(first)
    def _():
        x_copy(x_hbm, 0, 0).start()
        make_pool_maps(0, 0)

        def fetch(k, carry):
            c_in, c_out = w_copies(0, k, 0)
            c_in.start()
            c_out.start()
            c_in.wait()
            c_out.wait()
            cast_chunk(k, 0, 0)
            return carry
        lax.fori_loop(0, n_chunks, fetch, 0)

    x_copy(x_hbm, b, slot).wait()

    @pl.when(step >= 2)
    def _():
        o_copy(b, next_slot).wait()

    wraps = b + 1 == n_steps
    nb = jnp.where(wraps, 0, b + 1)
    nl = jnp.where(wraps, l + 1, l)

    @pl.when(jnp.logical_and(jnp.logical_not(last), nl == 0))
    def _():
        x_copy(x_hbm, nb, next_slot).start()

    @pl.when(jnp.logical_and(jnp.logical_not(last), nl > 0))
    def _():
        x_copy(o_hbm, nb, next_slot).start()

    more_layers = l + 1 < depth

    @pl.when(jnp.logical_and(more_layers, b < n_chunks))
    def _():
        c_in, c_out = w_copies(l + 1, b, b % 2)
        c_in.start()
        c_out.start()

    @pl.when(jnp.logical_and(more_layers, b == 0))
    def _():
        make_pool_maps(l + 1, 1 - wslot)

    @pl.when(jnp.logical_and(more_layers, b >= 1))
    def _():
        c_in, c_out = w_copies(l + 1, b - 1, (b - 1) % 2)
        c_in.wait()
        c_out.wait()
        cast_chunk(b - 1, (b - 1) % 2, 1 - wslot)

    @pl.when(jnp.logical_and(l >= 1, b == 0))
    def _():
        c_in, c_out = w_copies(l, n_chunks - 1, (n_chunks - 1) % 2)
        c_in.wait()
        c_out.wait()
        cast_chunk(n_chunks - 1, (n_chunks - 1) % 2, wslot)

    zc_ref[...] = jnp.zeros((HALO, cw), jnp.float32)
    pc_ref[...] = jnp.zeros((HALO, pw), jnp.float32)

    x_ref = xbuf.at[slot]
    wi_ref = win_ref.at[wslot]
    wo_ref = wout_ref.at[wslot]
    wp_ref = wpair_ref.at[wslot]
    wc_ref = wconv_ref.at[l]

    mod = _pick_row(mod_ref, b)
    shift = mod[:, 0:dm]
    gain = _pick_row(gpre_ref, l) * (1.0 + mod[:, dm:2 * dm])
    out_gain = mod[:, 2 * dm:3 * dm] * _pick_row(gpost_ref, l)

    def norm(i):
        x = x_ref[i * sub:(i + 1) * sub, :]
        r = lax.rsqrt(jnp.mean(x * x, axis=-1, keepdims=True) + NORM_EPS)
        h_ref[i % 2] = ((x * r) * gain + shift).astype(jnp.bfloat16)

    def project(i, k):
        return jnp.dot(h_ref[i % 2], wi_ref[:, k * blk:(k + 1) * blk], preferred_element_type=jnp.float32)

    def lagged(e, lag):
        return pltpu.roll(e, lag, axis=0)[HALO:, :]

    def mix_conv(i, d, res):
        cols = slice(d * gd, (d + 1) * gd)
        u, c, b_a, gate = (res[:, k * gd:(k + 1) * gd] for k in range(4))
        z = c * u
        ez = jnp.concatenate([zc_ref[:, cols], z], axis=0)
        zc_ref[:, cols] = z[sub - HALO:, :]
        conv = wc_ref[CONV_K - 1:CONV_K, cols] * z
        for k in range(CONV_K - 1):
            conv = conv + wc_ref[k:k + 1, cols] * lagged(ez, CONV_K - 1 - k)
        y_ref[i % 2, :, cols] = (b_a * conv * _silu(gate)).astype(jnp.bfloat16)

    def mix_pool(i, pair, res):
        pooled = []
        for k in range(2):
            g = 2 * pair + k
            w = POOL_WINDOWS[g]
            cols = slice(g * gd, (g + 1) * gd)
            v = res[:, k * gd:(k + 1) * gd]
            acc = jnp.concatenate([pc_ref[:, cols], v], axis=0)
            pc_ref[:, cols] = v[sub - HALO:, :]
            span = 1
            while span < min(w, 8):
                acc = acc + pltpu.roll(acc, span, axis=0)
                span *= 2
            if span < w:
                acc = acc[8:, :] + acc[:-8, :]
                acc = acc[HALO - 8:, :]
            else:
                acc = acc[HALO:, :]
            if i == 0:
                pos = (lax.broadcasted_iota(jnp.int32, (HALO, gd), 0) + 1).astype(jnp.float32)
                inv = jnp.concatenate([1.0 / jnp.minimum(pos, float(w)),
                                       jnp.full((sub - HALO, gd), 1.0 / w, jnp.float32)], axis=0)
            else:
                inv = 1.0 / w
            pooled.append(acc * inv - v)
        mapped = jnp.dot(jnp.concatenate(pooled, axis=-1).astype(jnp.bfloat16), wp_ref[pair],
                         preferred_element_type=jnp.float32)
        return mapped, res[:, 2 * gd:4 * gd]

    def gate_pool(i, pair, mapped, gate):
        y_ref[i % 2, :, cw + pair * 2 * gd:cw + (pair + 1) * 2 * gd] = (
            mapped * _silu(gate)).astype(jnp.bfloat16)

    def finish(i):
        rows = slice(i * sub, (i + 1) * sub)
        yo = jnp.dot(y_ref[i % 2], wo_ref[...], preferred_element_type=jnp.float32)
        r2 = lax.rsqrt(jnp.mean(yo * yo, axis=-1, keepdims=True) + NORM_EPS)
        x_ref[rows, :] = x_ref[rows, :] + (yo * r2) * out_gain

    assert n_conv >= 2 and n_pair == 2
    n_chains = tt // sub
    seq = list(range(n_conv, n_conv + n_pair)) + list(range(n_conv))
    norm(0)
    head = [project(0, seq[0]), project(0, seq[1])]
    for i in range(n_chains):
        more = i + 1 < n_chains
        if more:
            norm(i + 1)
        res = head
        head = []
        mapped = []
        for pos, k in enumerate(seq):
            if pos + 2 < len(seq):
                res.append(project(i, seq[pos + 2]))
            elif more:
                head.append(project(i + 1, seq[pos + 2 - len(seq)]))
            if k >= n_conv:
                mapped.append(mix_pool(i, k - n_conv, res[pos]))
            else:
                mix_conv(i, k, res[pos])
                if k < n_pair:
                    gate_pool(i, k, *mapped[k])
        finish(i)

    o_copy(b, slot).start()

    @pl.when(last)
    def _():
        o_copy(b, prev_slot).wait()
        o_copy(b, slot).wait()


def _trunk(x, mod, g_pre, g_post, w_in, w_conv, w_pool, pool_scale, w_out):
    batch, seq, d = x.shape
    depth, _, in_cols = w_in.shape
    cw = w_conv.shape[2]
    _, groups, gd, _ = w_pool.shape
    pw = groups * gd
    chunk = WEIGHT_CHUNK_ROWS
    assert w_in.shape[1] == w_out.shape[1] == d and d % chunk == 0
    assert cw == pw and cw % gd == 0 and groups % 2 == 0
    assert d // chunk == batch and seq % SUB_ROWS == 0 and batch * depth >= RING
    whole2 = lambda l, b: (0, 0)
    hbm = pl.BlockSpec(memory_space=pl.ANY)
    return pl.pallas_call(
        functools.partial(_trunk_kernel, cw=cw, gd=gd, sub=SUB_ROWS),
        grid=(depth, batch),
        in_specs=[
            hbm,
            pl.BlockSpec((None, batch, 3 * d), lambda l, b: (l, 0, 0)),
            pl.BlockSpec((depth, d), whole2),
            pl.BlockSpec((depth, d), whole2),
            hbm,
            pl.BlockSpec(w_conv.shape, lambda l, b: (0, 0, 0)),
            pl.BlockSpec(w_pool.shape, lambda l, b: (0, 0, 0, 0)),
            pl.BlockSpec((depth, pw), whole2),
            hbm,
        ],
        out_specs=hbm,
        out_shape=jax.ShapeDtypeStruct(x.shape, x.dtype),
        scratch_shapes=[
            pltpu.VMEM((RING, seq, d), jnp.float32),
            pltpu.VMEM((2, d, in_cols), jnp.bfloat16),
            pltpu.VMEM((2, d, w_out.shape[2]), jnp.bfloat16),
            pltpu.VMEM((2, groups // 2, 2 * gd, 2 * gd), jnp.bfloat16),
            pltpu.VMEM((2, chunk, in_cols), jnp.float32),
            pltpu.VMEM((2, chunk, w_out.shape[2]), jnp.float32),
            pltpu.VMEM((HALO, cw), jnp.float32),
            pltpu.VMEM((HALO, pw), jnp.float32),
            pltpu.VMEM((2, SUB_ROWS, d), jnp.bfloat16),
            pltpu.VMEM((2, SUB_ROWS, cw + pw), jnp.bfloat16),
            pltpu.SemaphoreType.DMA((RING,)),
            pltpu.SemaphoreType.DMA((RING,)),
            pltpu.SemaphoreType.DMA((2, 2)),
        ],
        compiler_params=pltpu.CompilerParams(
            dimension_semantics=("arbitrary", "arbitrary"),
            vmem_limit_bytes=VMEM_LIMIT_BYTES),
        name="mixer_trunk",
    )(x, mod, g_pre, g_post, w_in, w_conv, w_pool, pool_scale, w_out)


def kernel(x, c, w_ada, b_ada, g_pre, w_in, w_conv, w_pool, pool_scale, w_out, g_post):
    mod = _ada_modulation(c, w_ada, b_ada)
    return _trunk(x, mod, g_pre, g_post, w_in, w_conv, w_pool, pool_scale, w_out)
```

```python
import functools

import jax
import jax.numpy as jnp
from jax import lax
from jax.experimental import pallas as pl
from jax.experimental.pallas import tpu as pltpu

CONV_K = 3
POOL_WINDOWS = (2, 4, 8, 16)
NORM_EPS = 1e-6
HALO = 16
SUB_ROWS = 512
WEIGHT_CHUNK_ROWS = 128
RING = 3
VMEM_LIMIT_BYTES = 56 * 1024 * 1024


def _silu(v):
    return v * jax.nn.sigmoid(v)


def _pick_row(ref, idx):
    a = ref[...]
    rows = lax.broadcasted_iota(jnp.int32, a.shape, 0)
    return jnp.sum(jnp.where(rows == idx, a, 0.0), axis=0, keepdims=True)


def _ada_kernel(c_ref, w_ref, b_ref, o_ref):
    l = pl.program_id(0)
    c_act = _silu(c_ref[...]).astype(jnp.bfloat16)
    w = w_ref[...].astype(jnp.bfloat16)
    o_ref[...] = jnp.dot(c_act, w, preferred_element_type=jnp.float32) + b_ref[pl.ds(l, 1), :]


def _ada_modulation(c, w_ada, b_ada):
    depth, d, cols = w_ada.shape
    batch = c.shape[0]
    return pl.pallas_call(
        _ada_kernel,
        grid=(depth,),
        in_specs=[
            pl.BlockSpec((batch, d), lambda l: (0, 0)),
            pl.BlockSpec((None, d, cols), lambda l: (l, 0, 0)),
            pl.BlockSpec((depth, cols), lambda l: (0, 0)),
        ],
        out_specs=pl.BlockSpec((None, batch, cols), lambda l: (l, 0, 0)),
        out_shape=jax.ShapeDtypeStruct((depth, batch, cols), jnp.float32),
        compiler_params=pltpu.CompilerParams(
            dimension_semantics=("arbitrary",),
            vmem_limit_bytes=VMEM_LIMIT_BYTES),
        name="ada_modulation",
    )(c, w_ada, b_ada)


def _proj_block_order(cw, pw, gd):
    nb = cw // gd
    order = []
    for d in range(nb):
        order += [0 * nb + d, 2 * nb + d, 1 * nb + d, 3 * nb + d]
    for g in range(0, pw // gd, 2):
        order += [4 * nb + g, 4 * nb + g + 1, 4 * nb + pw // gd + g, 4 * nb + pw // gd + g + 1]
    return order


def _trunk_kernel(x_hbm, mod_ref, gpre_ref, gpost_ref, win_hbm, wconv_ref, wpool_ref, pscale_ref,
                  wout_hbm, o_hbm, xbuf, win_ref, wout_ref, wpair_ref, stage_in, stage_out,
                  zc_ref, pc_ref, h_ref, y_ref, sem_x, sem_o, sem_w, *, cw, gd, sub):
    l = pl.program_id(0)
    b = pl.program_id(1)
    depth = pl.num_programs(0)
    n_steps = pl.num_programs(1)
    _, tt, dm = xbuf.shape
    n_pool = len(POOL_WINDOWS)
    pw = gd * n_pool
    n_conv = cw // gd
    n_pair = n_pool // 2
    blk = 4 * gd
    order = _proj_block_order(cw, pw, gd)
    chunk = WEIGHT_CHUNK_ROWS
    n_chunks = win_hbm.shape[1] // chunk

    wslot = l % 2
    step = l * n_steps + b
    slot = step % RING
    next_slot = (step + 1) % RING
    prev_slot = (step + RING - 1) % RING
    first = step == 0
    last = step == depth * n_steps - 1

    def x_copy(src_hbm, bb, sl):
        return pltpu.make_async_copy(src_hbm.at[bb], xbuf.at[sl], sem_x.at[sl])

    def o_copy(bb, sl):
        return pltpu.make_async_copy(xbuf.at[sl], o_hbm.at[bb], sem_o.at[sl])

    def w_copies(layer, k, st):
        rows = pl.ds(pl.multiple_of(k * chunk, chunk), chunk)
        return (pltpu.make_async_copy(win_hbm.at[layer, rows, :], stage_in.at[st], sem_w.at[0, st]),
                pltpu.make_async_copy(wout_hbm.at[layer, rows, :], stage_out.at[st], sem_w.at[1, st]))

    def cast_chunk(k, st, ws):
        rows = pl.ds(pl.multiple_of(k * chunk, chunk), chunk)
        for q, p in enumerate(order):
            win_ref[ws, rows, q * gd:(q + 1) * gd] = stage_in[st, :, p * gd:(p + 1) * gd].astype(jnp.bfloat16)
        wout_ref[ws, rows, :] = stage_out[st].astype(jnp.bfloat16)

    def make_pool_maps(layer, ws):
        zero = jnp.zeros((gd, gd), jnp.float32)
        scale = _pick_row(pscale_ref, layer)
        for g in range(0, n_pool, 2):
            scaled = [wpool_ref[layer, g + k] * scale[:, (g + k) * gd:(g + k + 1) * gd] for k in range(2)]
            wpair_ref[ws, g // 2] = jnp.concatenate(
                [jnp.concatenate([scaled[0], zero], axis=-1),
                 jnp.concatenate([zero, scaled[1]], axis=-1)], axis=0).astype(jnp.bfloat16)

    @pl.when(first)
    def _():
        x_copy(x_hbm, 0, 0).start()
        make_pool_maps(0, 0)

        def fetch(k, carry):
            c_in, c_out = w_copies(0, k, 0)
            c_in.start()
            c_out.start()
            c_in.wait()
            c_out.wait()
            cast_chunk(k, 0, 0)
            return carry
        lax.fori_loop(0, n_chunks, fetch, 0)

    x_copy(x_hbm, b, slot).wait()

    @pl.when(step >= 2)
    def _():
        o_copy(b, next_slot).wait()

    wraps = b + 1 == n_steps
    nb = jnp.where(wraps, 0, b + 1)
    nl = jnp.where(wraps, l + 1, l)

    @pl.when(jnp.logical_and(jnp.logical_not(last), nl == 0))
    def _():
        x_copy(x_hbm, nb, next_slot).start()

    @pl.when(jnp.logical_and(jnp.logical_not(last), nl > 0))
    def _():
        x_copy(o_hbm, nb, next_slot).start()

    more_layers = l + 1 < depth

    @pl.when(jnp.logical_and(more_layers, b < n_chunks))
    def _():
        c_in, c_out = w_copies(l + 1, b, b % 2)
        c_in.start()
        c_out.start()

    @pl.when(jnp.logical_and(more_layers, b == 0))
    def _():
        make_pool_maps(l + 1, 1 - wslot)

    @pl.when(jnp.logical_and(more_layers, b >= 1))
    def _():
        c_in, c_out = w_copies(l + 1, b - 1, (b - 1) % 2)
        c_in.wait()
        c_out.wait()
        cast_chunk(b - 1, (b - 1) % 2, 1 - wslot)

    @pl.when(jnp.logical_and(l >= 1, b == 0))
    def _():
        c_in, c_out = w_copies(l, n_chunks - 1, (n_chunks - 1) % 2)
        c_in.wait()
        c_out.wait()
        cast_chunk(n_chunks - 1, (n_chunks - 1) % 2, wslot)

    zc_ref[...] = jnp.zeros((HALO, cw), jnp.float32)
    pc_ref[...] = jnp.zeros((HALO, pw), jnp.float32)

    x_ref = xbuf.at[slot]
    wi_ref = win_ref.at[wslot]
    wo_ref = wout_ref.at[wslot]
    wp_ref = wpair_ref.at[wslot]
    wc_ref = wconv_ref.at[l]

    mod = _pick_row(mod_ref, b)
    shift = mod[:, 0:dm]
    gain = _pick_row(gpre_ref, l) * (1.0 + mod[:, dm:2 * dm])
    out_gain = mod[:, 2 * dm:3 * dm] * _pick_row(gpost_ref, l)

    def norm(i):
        x = x_ref[i * sub:(i + 1) * sub, :]
        r = lax.rsqrt(jnp.mean(x * x, axis=-1, keepdims=True) + NORM_EPS)
        h_ref[i % 2] = ((x * r) * gain + shift).astype(jnp.bfloat16)

    def project(i, k):
        return jnp.dot(h_ref[i % 2], wi_ref[:, k * blk:(k + 1) * blk], preferred_element_type=jnp.float32)

    def lagged(e, lag):
        return pltpu.roll(e, lag, axis=0)[HALO:, :]

    def mix_conv(i, d, res):
        cols = slice(d * gd, (d + 1) * gd)
        u, c, b_a, gate = (res[:, k * gd:(k + 1) * gd] for k in range(4))
        z = c * u
        ez = jnp.concatenate([zc_ref[:, cols], z], axis=0)
        zc_ref[:, cols] = z[sub - HALO:, :]
        conv = wc_ref[CONV_K - 1:CONV_K, cols] * z
        for k in range(CONV_K - 1):
            conv = conv + wc_ref[k:k + 1, cols] * lagged(ez, CONV_K - 1 - k)
        y_ref[i % 2, :, cols] = (b_a * conv * _silu(gate)).astype(jnp.bfloat16)

    def mix_pool(i, pair, res):
        pooled = []
        for k in range(2):
            g = 2 * pair + k
            w = POOL_WINDOWS[g]
            cols = slice(g * gd, (g + 1) * gd)
            v = res[:, k * gd:(k + 1) * gd]
            acc = jnp.concatenate([pc_ref[:, cols], v], axis=0)
            pc_ref[:, cols] = v[sub - HALO:, :]
            span = 1
            while span < min(w, 8):
                acc = acc + pltpu.roll(acc, span, axis=0)
                span *= 2
            if span < w:
                acc = acc[8:, :] + acc[:-8, :]
                acc = acc[HALO - 8:, :]
            else:
                acc = acc[HALO:, :]
            if i == 0:
                pos = (lax.broadcasted_iota(jnp.int32, (HALO, gd), 0) + 1).astype(jnp.float32)
                inv = jnp.concatenate([1.0 / jnp.minimum(pos, float(w)),
                                       jnp.full((sub - HALO, gd), 1.0 / w, jnp.float32)], axis=0)
            else:
                inv = 1.0 / w
            pooled.append(acc * inv - v)
        mapped = jnp.dot(jnp.concatenate(pooled, axis=-1).astype(jnp.bfloat16), wp_ref[pair],
                         preferred_element_type=jnp.float32)
        return mapped, res[:, 2 * gd:4 * gd]

    def gate_pool(i, pair, mapped, gate):
        y_ref[i % 2, :, cw + pair * 2 * gd:cw + (pair + 1) * 2 * gd] = (
            mapped * _silu(gate)).astype(jnp.bfloat16)

    def finish(i):
        rows = slice(i * sub, (i + 1) * sub)
        yo = jnp.dot(y_ref[i % 2], wo_ref[...], preferred_element_type=jnp.float32)
        r2 = lax.rsqrt(jnp.mean(yo * yo, axis=-1, keepdims=True) + NORM_EPS)
        x_ref[rows, :] = x_ref[rows, :] + (yo * r2) * out_gain

    assert n_conv >= 2 and n_pair == 2
    n_chains = tt // sub
    seq = list(range(n_conv, n_conv + n_pair)) + list(range(n_conv))
    norm(0)
    head = [project(0, seq[0]), project(0, seq[1])]
    for i in range(n_chains):
        more = i + 1 < n_chains
        if more:
            norm(i + 1)
        res = head
        head = []
        mapped = []
        for pos, k in enumerate(seq):
            if pos + 2 < len(seq):
                res.append(project(i, seq[pos + 2]))
            elif more:
                head.append(project(i + 1, seq[pos + 2 - len(seq)]))
            if k >= n_conv:
                mapped.append(mix_pool(i, k - n_conv, res[pos]))
            else:
                mix_conv(i, k, res[pos])
                if k < n_pair:
                    gate_pool(i, k, *mapped[k])
        finish(i)

    o_copy(b, slot).start()

    @pl.when(last)
    def _():
        o_copy(b, prev_slot).wait()
        o_copy(b, slot).wait()


def _trunk(x, mod, g_pre, g_post, w_in, w_conv, w_pool, pool_scale, w_out):
    batch, seq, d = x.shape
    depth, _, in_cols = w_in.shape
    cw = w_conv.shape[2]
    _, groups, gd, _ = w_pool.shape
    pw = groups * gd
    chunk = WEIGHT_CHUNK_ROWS
    assert w_in.shape[1] == w_out.shape[1] == d and d % chunk == 0
    assert cw == pw and cw % gd == 0 and groups % 2 == 0
    assert d // chunk == batch and seq % SUB_ROWS == 0 and batch * depth >= RING
    whole2 = lambda l, b: (0, 0)
    hbm = pl.BlockSpec(memory_space=pl.ANY)
    return pl.pallas_call(
        functools.partial(_trunk_kernel, cw=cw, gd=gd, sub=SUB_ROWS),
        grid=(depth, batch),
        in_specs=[
            hbm,
            pl.BlockSpec((None, batch, 3 * d), lambda l, b: (l, 0, 0)),
            pl.BlockSpec((depth, d), whole2),
            pl.BlockSpec((depth, d), whole2),
            hbm,
            pl.BlockSpec(w_conv.shape, lambda l, b: (0, 0, 0)),
            pl.BlockSpec(w_pool.shape, lambda l, b: (0, 0, 0, 0)),
            pl.BlockSpec((depth, pw), whole2),
            hbm,
        ],
        out_specs=hbm,
        out_shape=jax.ShapeDtypeStruct(x.shape, x.dtype),
        scratch_shapes=[
            pltpu.VMEM((RING, seq, d), jnp.float32),
            pltpu.VMEM((2, d, in_cols), jnp.bfloat16),
            pltpu.VMEM((2, d, w_out.shape[2]), jnp.bfloat16),
            pltpu.VMEM((2, groups // 2, 2 * gd, 2 * gd), jnp.bfloat16),
            pltpu.VMEM((2, chunk, in_cols), jnp.float32),
            pltpu.VMEM((2, chunk, w_out.shape[2]), jnp.float32),
            pltpu.VMEM((HALO, cw), jnp.float32),
            pltpu.VMEM((HALO, pw), jnp.float32),
            pltpu.VMEM((2, SUB_ROWS, d), jnp.bfloat16),
            pltpu.VMEM((2, SUB_ROWS, cw + pw), jnp.bfloat16),
            pltpu.SemaphoreType.DMA((RING,)),
            pltpu.SemaphoreType.DMA((RING,)),
            pltpu.SemaphoreType.DMA((2, 2)),
        ],
        compiler_params=pltpu.CompilerParams(
            dimension_semantics=("arbitrary", "arbitrary"),
            vmem_limit_bytes=VMEM_LIMIT_BYTES),
        name="mixer_trunk",
    )(x, mod, g_pre, g_post, w_in, w_conv, w_pool, pool_scale, w_out)


def kernel(x, c, w_ada, b_ada, g_pre, w_in, w_conv, w_pool, pool_scale, w_out, g_post):
    mod = _ada_modulation(c, w_ada, b_ada)
    return _trunk(x, mod, g_pre, g_post, w_in, w_conv, w_pool, pool_scale, w_out)
```

```python
import functools

import jax
import jax.numpy as jnp
from jax import lax
from jax.experimental import pallas as pl
from jax.experimental.pallas import tpu as pltpu

CONV_K = 3
POOL_WINDOWS = (2, 4, 8, 16)
NORM_EPS = 1e-6
HALO = 16
SUB_ROWS = 512
WEIGHT_CHUNK_ROWS = 128
RING = 3
VMEM_LIMIT_BYTES = 56 * 1024 * 1024


def _silu(v):
    return v * jax.nn.sigmoid(v)


def _pick_row(ref, idx):
    a = ref[...]
    rows = lax.broadcasted_iota(jnp.int32, a.shape, 0)
    return jnp.sum(jnp.where(rows == idx, a, 0.0), axis=0, keepdims=True)


def _ada_kernel(c_ref, w_ref, b_ref, o_ref):
    l = pl.program_id(0)
    c_act = _silu(c_ref[...]).astype(jnp.bfloat16)
    w = w_ref[...].astype(jnp.bfloat16)
    o_ref[...] = jnp.dot(c_act, w, preferred_element_type=jnp.float32) + b_ref[pl.ds(l, 1), :]


def _ada_modulation(c, w_ada, b_ada):
    depth, d, cols = w_ada.shape
    batch = c.shape[0]
    return pl.pallas_call(
        _ada_kernel,
        grid=(depth,),
        in_specs=[
            pl.BlockSpec((batch, d), lambda l: (0, 0)),
            pl.BlockSpec((None, d, cols), lambda l: (l, 0, 0)),
            pl.BlockSpec((depth, cols), lambda l: (0, 0)),
        ],
        out_specs=pl.BlockSpec((None, batch, cols), lambda l: (l, 0, 0)),
        out_shape=jax.ShapeDtypeStruct((depth, batch, cols), jnp.float32),
        compiler_params=pltpu.CompilerParams(
            dimension_semantics=("arbitrary",),
            vmem_limit_bytes=VMEM_LIMIT_BYTES),
        name="ada_modulation",
    )(c, w_ada, b_ada)


def _proj_block_order(cw, pw, gd):
    nb = cw // gd
    order = []
    for d in range(nb):
        order += [0 * nb + d, 2 * nb + d, 1 * nb + d, 3 * nb + d]
    for g in range(0, pw // gd, 2):
        order += [4 * nb + g, 4 * nb + g + 1, 4 * nb + pw // gd + g, 4 * nb + pw // gd + g + 1]
    return order


def _trunk_kernel(x_hbm, mod_ref, gpre_ref, gpost_ref, win_hbm, wconv_ref, wpool_ref, pscale_ref,
                  wout_hbm, o_hbm, xbuf, win_ref, wout_ref, wpair_ref, stage_in, stage_out,
                  zc_ref, pc_ref, h_ref, y_ref, sem_x, sem_o, sem_w, *, cw, gd, sub):
    l = pl.program_id(0)
    b = pl.program_id(1)
    depth = pl.num_programs(0)
    n_steps = pl.num_programs(1)
    _, tt, dm = xbuf.shape
    n_pool = len(POOL_WINDOWS)
    pw = gd * n_pool
    n_conv = cw // gd
    n_pair = n_pool // 2
    blk = 4 * gd
    order = _proj_block_order(cw, pw, gd)
    chunk = WEIGHT_CHUNK_ROWS
    n_chunks = win_hbm.shape[1] // chunk

    wslot = l % 2
    step = l * n_steps + b
    slot = step % RING
    next_slot = (step + 1) % RING
    prev_slot = (step + RING - 1) % RING
    first = step == 0
    last = step == depth * n_steps - 1

    def x_copy(src_hbm, bb, sl):
        return pltpu.make_async_copy(src_hbm.at[bb], xbuf.at[sl], sem_x.at[sl])

    def o_copy(bb, sl):
        return pltpu.make_async_copy(xbuf.at[sl], o_hbm.at[bb], sem_o.at[sl])

    def w_copies(layer, k, st):
        rows = pl.ds(pl.multiple_of(k * chunk, chunk), chunk)
        return (pltpu.make_async_copy(win_hbm.at[layer, rows, :], stage_in.at[st], sem_w.at[0, st]),
                pltpu.make_async_copy(wout_hbm.at[layer, rows, :], stage_out.at[st], sem_w.at[1, st]))

    def cast_chunk(k, st, ws):
        rows = pl.ds(pl.multiple_of(k * chunk, chunk), chunk)
        for q, p in enumerate(order):
            win_ref[ws, rows, q * gd:(q + 1) * gd] = stage_in[st, :, p * gd:(p + 1) * gd].astype(jnp.bfloat16)
        wout_ref[ws, rows, :] = stage_out[st].astype(jnp.bfloat16)

    def make_pool_maps(layer, ws):
        zero = jnp.zeros((gd, gd), jnp.float32)
        scale = _pick_row(pscale_ref, layer)
        for g in range(0, n_pool, 2):
            scaled = [wpool_ref[layer, g + k] * scale[:, (g + k) * gd:(g + k + 1) * gd] for k in range(2)]
            wpair_ref[ws, g // 2] = jnp.concatenate(
                [jnp.concatenate([scaled[0], zero], axis=-1),
                 jnp.concatenate([zero, scaled[1]], axis=-1)], axis=0).astype(jnp.bfloat16)

    @pl.when(first)
    def _():
        x_copy(x_hbm, 0, 0).start()
        make_pool_maps(0, 0)

        for c in w_copies(0, 0, 0):
            c.start()

        def fetch(k, carry):
            @pl.when(k + 1 < n_chunks)
            def _():
                for c in w_copies(0, k + 1, (k + 1) % 2):
                    c.start()
            for c in w_copies(0, k, k % 2):
                c.wait()
            cast_chunk(k, k % 2, 0)
            return carry
        lax.fori_loop(0, n_chunks, fetch, 0)

    x_copy(x_hbm, b, slot).wait()

    @pl.when(step >= 2)
    def _():
        o_copy(b, next_slot).wait()

    wraps = b + 1 == n_steps
    nb = jnp.where(wraps, 0, b + 1)
    nl = jnp.where(wraps, l + 1, l)

    @pl.when(jnp.logical_and(jnp.logical_not(last), nl == 0))
    def _():
        x_copy(x_hbm, nb, next_slot).start()

    @pl.when(jnp.logical_and(jnp.logical_not(last), nl > 0))
    def _():
        x_copy(o_hbm, nb, next_slot).start()

    more_layers = l + 1 < depth

    @pl.when(jnp.logical_and(more_layers, b < n_chunks))
    def _():
        c_in, c_out = w_copies(l + 1, b, b % 2)
        c_in.start()
        c_out.start()

    @pl.when(jnp.logical_and(more_layers, b == 0))
    def _():
        make_pool_maps(l + 1, 1 - wslot)

    @pl.when(jnp.logical_and(more_layers, b >= 1))
    def _():
        c_in, c_out = w_copies(l + 1, b - 1, (b - 1) % 2)
        c_in.wait()
        c_out.wait()
        cast_chunk(b - 1, (b - 1) % 2, 1 - wslot)

    @pl.when(jnp.logical_and(l >= 1, b == 0))
    def _():
        c_in, c_out = w_copies(l, n_chunks - 1, (n_chunks - 1) % 2)
        c_in.wait()
        c_out.wait()
        cast_chunk(n_chunks - 1, (n_chunks - 1) % 2, wslot)

    zc_ref[...] = jnp.zeros((HALO, cw), jnp.float32)
    pc_ref[...] = jnp.zeros((HALO, pw), jnp.float32)

    x_ref = xbuf.at[slot]
    wi_ref = win_ref.at[wslot]
    wo_ref = wout_ref.at[wslot]
    wp_ref = wpair_ref.at[wslot]
    wc_ref = wconv_ref.at[l]

    mod = _pick_row(mod_ref, b)
    shift = mod[:, 0:dm]
    gain = _pick_row(gpre_ref, l) * (1.0 + mod[:, dm:2 * dm])
    out_gain = mod[:, 2 * dm:3 * dm] * _pick_row(gpost_ref, l)

    def norm(i):
        x = x_ref[i * sub:(i + 1) * sub, :]
        r = lax.rsqrt(jnp.mean(x * x, axis=-1, keepdims=True) + NORM_EPS)
        h_ref[i % 2] = ((x * r) * gain + shift).astype(jnp.bfloat16)

    def project(i, k):
        return jnp.dot(h_ref[i % 2], wi_ref[:, k * blk:(k + 1) * blk], preferred_element_type=jnp.float32)

    def lagged(e, lag):
        return pltpu.roll(e, lag, axis=0)[HALO:, :]

    def mix_conv(i, d, res):
        cols = slice(d * gd, (d + 1) * gd)
        u, c, b_a, gate = (res[:, k * gd:(k + 1) * gd] for k in range(4))
        z = c * u
        ez = jnp.concatenate([zc_ref[:, cols], z], axis=0)
        zc_ref[:, cols] = z[sub - HALO:, :]
        conv = wc_ref[CONV_K - 1:CONV_K, cols] * z
        for k in range(CONV_K - 1):
            conv = conv + wc_ref[k:k + 1, cols] * lagged(ez, CONV_K - 1 - k)
        y_ref[i % 2, :, cols] = (b_a * conv * _silu(gate)).astype(jnp.bfloat16)

    def mix_pool(i, pair, res):
        pooled = []
        for k in range(2):
            g = 2 * pair + k
            w = POOL_WINDOWS[g]
            cols = slice(g * gd, (g + 1) * gd)
            v = res[:, k * gd:(k + 1) * gd]
            acc = jnp.concatenate([pc_ref[:, cols], v], axis=0)
            pc_ref[:, cols] = v[sub - HALO:, :]
            span = 1
            while span < min(w, 8):
                acc = acc + pltpu.roll(acc, span, axis=0)
                span *= 2
            if span < w:
                acc = acc[8:, :] + acc[:-8, :]
                acc = acc[HALO - 8:, :]
            else:
                acc = acc[HALO:, :]
            if i == 0:
                pos = (lax.broadcasted_iota(jnp.int32, (HALO, gd), 0) + 1).astype(jnp.float32)
                inv = jnp.concatenate([1.0 / jnp.minimum(pos, float(w)),
                                       jnp.full((sub - HALO, gd), 1.0 / w, jnp.float32)], axis=0)
            else:
                inv = 1.0 / w
            pooled.append(acc * inv - v)
        mapped = jnp.dot(jnp.concatenate(pooled, axis=-1).astype(jnp.bfloat16), wp_ref[pair],
                         preferred_element_type=jnp.float32)
        return mapped, res[:, 2 * gd:4 * gd]

    def gate_pool(i, pair, mapped, gate):
        y_ref[i % 2, :, cw + pair * 2 * gd:cw + (pair + 1) * 2 * gd] = (
            mapped * _silu(gate)).astype(jnp.bfloat16)

    def finish(i):
        rows = slice(i * sub, (i + 1) * sub)
        yo = jnp.dot(y_ref[i % 2], wo_ref[...], preferred_element_type=jnp.float32)
        r2 = lax.rsqrt(jnp.mean(yo * yo, axis=-1, keepdims=True) + NORM_EPS)
        x_ref[rows, :] = x_ref[rows, :] + (yo * r2) * out_gain

    assert n_conv >= 2 and n_pair == 2
    n_chains = tt // sub
    seq = list(range(n_conv, n_conv + n_pair)) + list(range(n_conv))
    norm(0)
    head = [project(0, seq[0]), project(0, seq[1])]
    for i in range(n_chains):
        more = i + 1 < n_chains
        if more:
            norm(i + 1)
        res = head
        head = []
        mapped = []
        for pos, k in enumerate(seq):
            if pos + 2 < len(seq):
                res.append(project(i, seq[pos + 2]))
            elif more:
                head.append(project(i + 1, seq[pos + 2 - len(seq)]))
            if k >= n_conv:
                mapped.append(mix_pool(i, k - n_conv, res[pos]))
            else:
                mix_conv(i, k, res[pos])
                if k < n_pair:
                    gate_pool(i, k, *mapped[k])
        finish(i)

    o_copy(b, slot).start()

    @pl.when(last)
    def _():
        o_copy(b, prev_slot).wait()
        o_copy(b, slot).wait()


def _trunk(x, mod, g_pre, g_post, w_in, w_conv, w_pool, pool_scale, w_out):
    batch, seq, d = x.shape
    depth, _, in_cols = w_in.shape
    cw = w_conv.shape[2]
    _, groups, gd, _ = w_pool.shape
    pw = groups * gd
    chunk = WEIGHT_CHUNK_ROWS
    assert w_in.shape[1] == w_out.shape[1] == d and d % chunk == 0
    assert cw == pw and cw % gd == 0 and groups % 2 == 0
    assert d // chunk == batch and seq % SUB_ROWS == 0 and batch * depth >= RING
    whole2 = lambda l, b: (0, 0)
    hbm = pl.BlockSpec(memory_space=pl.ANY)
    return pl.pallas_call(
        functools.partial(_trunk_kernel, cw=cw, gd=gd, sub=SUB_ROWS),
        grid=(depth, batch),
        in_specs=[
            hbm,
            pl.BlockSpec((None, batch, 3 * d), lambda l, b: (l, 0, 0)),
            pl.BlockSpec((depth, d), whole2),
            pl.BlockSpec((depth, d), whole2),
            hbm,
            pl.BlockSpec(w_conv.shape, lambda l, b: (0, 0, 0)),
            pl.BlockSpec(w_pool.shape, lambda l, b: (0, 0, 0, 0)),
            pl.BlockSpec((depth, pw), whole2),
            hbm,
        ],
        out_specs=hbm,
        out_shape=jax.ShapeDtypeStruct(x.shape, x.dtype),
        scratch_shapes=[
            pltpu.VMEM((RING, seq, d), jnp.float32),
            pltpu.VMEM((2, d, in_cols), jnp.bfloat16),
            pltpu.VMEM((2, d, w_out.shape[2]), jnp.bfloat16),
            pltpu.VMEM((2, groups // 2, 2 * gd, 2 * gd), jnp.bfloat16),
            pltpu.VMEM((2, chunk, in_cols), jnp.float32),
            pltpu.VMEM((2, chunk, w_out.shape[2]), jnp.float32),
            pltpu.VMEM((HALO, cw), jnp.float32),
            pltpu.VMEM((HALO, pw), jnp.float32),
            pltpu.VMEM((2, SUB_ROWS, d), jnp.bfloat16),
            pltpu.VMEM((2, SUB_ROWS, cw + pw), jnp.bfloat16),
            pltpu.SemaphoreType.DMA((RING,)),
            pltpu.SemaphoreType.DMA((RING,)),
            pltpu.SemaphoreType.DMA((2, 2)),
        ],
        compiler_params=pltpu.CompilerParams(
            dimension_semantics=("arbitrary", "arbitrary"),
            vmem_limit_bytes=VMEM_LIMIT_BYTES),
        name="mixer_trunk",
    )(x, mod, g_pre, g_post, w_in, w_conv, w_pool, pool_scale, w_out)


def kernel(x, c, w_ada, b_ada, g_pre, w_in, w_conv, w_pool, pool_scale, w_out, g_post):
    mod = _ada_modulation(c, w_ada, b_ada)
    return _trunk(x, mod, g_pre, g_post, w_in, w_conv, w_pool, pool_scale, w_out)
```

```python
import functools

import jax
import jax.numpy as jnp
from jax import lax
from jax.experimental import pallas as pl
from jax.experimental.pallas import tpu as pltpu

CONV_K = 3
POOL_WINDOWS = (2, 4, 8, 16)
NORM_EPS = 1e-6
HALO = 16
SUB_ROWS = 512
WEIGHT_CHUNK_ROWS = 128
RING = 3
VMEM_LIMIT_BYTES = 56 * 1024 * 1024


def _silu(v):
    return v * jax.nn.sigmoid(v)


def _pick_row(ref, idx):
    a = ref[...]
    rows = lax.broadcasted_iota(jnp.int32, a.shape, 0)
    return jnp.sum(jnp.where(rows == idx, a, 0.0), axis=0, keepdims=True)


def _ada_kernel(c_ref, w_ref, b_ref, o_ref):
    l = pl.program_id(0)
    c_act = _silu(c_ref[...]).astype(jnp.bfloat16)
    w = w_ref[...].astype(jnp.bfloat16)
    o_ref[...] = jnp.dot(c_act, w, preferred_element_type=jnp.float32) + b_ref[pl.ds(l, 1), :]


def _ada_modulation(c, w_ada, b_ada):
    depth, d, cols = w_ada.shape
    batch = c.shape[0]
    return pl.pallas_call(
        _ada_kernel,
        grid=(depth,),
        in_specs=[
            pl.BlockSpec((batch, d), lambda l: (0, 0)),
            pl.BlockSpec((None, d, cols), lambda l: (l, 0, 0)),
            pl.BlockSpec((depth, cols), lambda l: (0, 0)),
        ],
        out_specs=pl.BlockSpec((None, batch, cols), lambda l: (l, 0, 0)),
        out_shape=jax.ShapeDtypeStruct((depth, batch, cols), jnp.float32),
        compiler_params=pltpu.CompilerParams(
            dimension_semantics=("arbitrary",),
            vmem_limit_bytes=VMEM_LIMIT_BYTES),
        name="ada_modulation",
    )(c, w_ada, b_ada)


def _proj_block_order(cw, pw, gd):
    nb = cw // gd
    order = []
    for d in range(nb):
        order += [0 * nb + d, 2 * nb + d, 1 * nb + d, 3 * nb + d]
    for g in range(0, pw // gd, 2):
        order += [4 * nb + g, 4 * nb + g + 1, 4 * nb + pw // gd + g, 4 * nb + pw // gd + g + 1]
    return order


def _trunk_kernel(x_hbm, mod_ref, gpre_ref, gpost_ref, win_hbm, wconv_ref, wpool_ref, pscale_ref,
                  wout_hbm, o_hbm, xbuf, win_ref, wout_ref, wpair_ref, stage_in, stage_out,
                  zc_ref, pc_ref, h_ref, y_ref, sem_x, sem_o, sem_w, *, cw, gd, sub):
    l = pl.program_id(0)
    b = pl.program_id(1)
    depth = pl.num_programs(0)
    n_steps = pl.num_programs(1)
    _, tt, dm = xbuf.shape
    n_pool = len(POOL_WINDOWS)
    pw = gd * n_pool
    n_conv = cw // gd
    n_pair = n_pool // 2
    blk = 4 * gd
    order = _proj_block_order(cw, pw, gd)
    chunk = WEIGHT_CHUNK_ROWS
    n_chunks = win_hbm.shape[1] // chunk

    wslot = l % 2
    step = l * n_steps + b
    slot = step % RING
    next_slot = (step + 1) % RING
    prev_slot = (step + RING - 1) % RING
    first = step == 0
    last = step == depth * n_steps - 1

    def x_copy(src_hbm, bb, sl):
        return pltpu.make_async_copy(src_hbm.at[bb], xbuf.at[sl], sem_x.at[sl])

    def o_copy(bb, sl):
        return pltpu.make_async_copy(xbuf.at[sl], o_hbm.at[bb], sem_o.at[sl])

    def w_copies(layer, k, st):
        rows = pl.ds(pl.multiple_of(k * chunk, chunk), chunk)
        return (pltpu.make_async_copy(win_hbm.at[layer, rows, :], stage_in.at[st], sem_w.at[0, st]),
                pltpu.make_async_copy(wout_hbm.at[layer, rows, :], stage_out.at[st], sem_w.at[1, st]))

    def cast_chunk(k, st, ws):
        rows = pl.ds(pl.multiple_of(k * chunk, chunk), chunk)
        for q, p in enumerate(order):
            win_ref[ws, rows, q * gd:(q + 1) * gd] = stage_in[st, :, p * gd:(p + 1) * gd].astype(jnp.bfloat16)
        wout_ref[ws, rows, :] = stage_out[st].astype(jnp.bfloat16)

    def make_pool_maps(layer, ws):
        zero = jnp.zeros((gd, gd), jnp.float32)
        scale = _pick_row(pscale_ref, layer)
        for g in range(0, n_pool, 2):
            scaled = [wpool_ref[layer, g + k] * scale[:, (g + k) * gd:(g + k + 1) * gd] for k in range(2)]
            wpair_ref[ws, g // 2] = jnp.concatenate(
                [jnp.concatenate([scaled[0], zero], axis=-1),
                 jnp.concatenate([zero, scaled[1]], axis=-1)], axis=0).astype(jnp.bfloat16)

    @pl.when(first)
    def _():
        x_copy(x_hbm, 0, 0).start()
        make_pool_maps(0, 0)

        for c in w_copies(0, 0, 0):
            c.start()

        def fetch(k, carry):
            @pl.when(k + 1 < n_chunks)
            def _():
                for c in w_copies(0, k + 1, (k + 1) % 2):
                    c.start()
            for c in w_copies(0, k, k % 2):
                c.wait()
            cast_chunk(k, k % 2, 0)
            return carry
        lax.fori_loop(0, n_chunks, fetch, 0)

    x_copy(x_hbm, b, slot).wait()

    @pl.when(step >= 2)
    def _():
        o_copy(b, next_slot).wait()

    wraps = b + 1 == n_steps
    nb = jnp.where(wraps, 0, b + 1)
    nl = jnp.where(wraps, l + 1, l)

    @pl.when(jnp.logical_and(jnp.logical_not(last), nl == 0))
    def _():
        x_copy(x_hbm, nb, next_slot).start()

    @pl.when(jnp.logical_and(jnp.logical_not(last), nl > 0))
    def _():
        x_copy(o_hbm, nb, next_slot).start()

    more_layers = l + 1 < depth

    @pl.when(jnp.logical_and(more_layers, b < n_chunks))
    def _():
        c_in, c_out = w_copies(l + 1, b, b % 2)
        c_in.start()
        c_out.start()

    @pl.when(jnp.logical_and(more_layers, b == 0))
    def _():
        make_pool_maps(l + 1, 1 - wslot)

    @pl.when(jnp.logical_and(more_layers, b >= 1))
    def _():
        c_in, c_out = w_copies(l + 1, b - 1, (b - 1) % 2)
        c_in.wait()
        c_out.wait()
        cast_chunk(b - 1, (b - 1) % 2, 1 - wslot)

    @pl.when(jnp.logical_and(l >= 1, b == 0))
    def _():
        c_in, c_out = w_copies(l, n_chunks - 1, (n_chunks - 1) % 2)
        c_in.wait()
        c_out.wait()
        cast_chunk(n_chunks - 1, (n_chunks - 1) % 2, wslot)

    zc_ref[...] = jnp.zeros((HALO, cw), jnp.float32)
    pc_ref[...] = jnp.zeros((HALO, pw), jnp.float32)

    x_ref = xbuf.at[slot]
    wi_ref = win_ref.at[wslot]
    wo_ref = wout_ref.at[wslot]
    wp_ref = wpair_ref.at[wslot]
    wc_ref = wconv_ref.at[l]

    mod = _pick_row(mod_ref, b)
    shift = mod[:, 0:dm]
    gain = _pick_row(gpre_ref, l) * (1.0 + mod[:, dm:2 * dm])
    out_gain = mod[:, 2 * dm:3 * dm] * _pick_row(gpost_ref, l)

    sizes = [sub // 2] + [sub] * ((tt - sub) // sub) + [sub // 2]
    starts = [sum(sizes[:i]) for i in range(len(sizes))]
    n_chains = len(sizes)

    def norm(i):
        x = x_ref[starts[i]:starts[i] + sizes[i], :]
        r = lax.rsqrt(jnp.mean(x * x, axis=-1, keepdims=True) + NORM_EPS)
        h_ref[i % 2, 0:sizes[i], :] = ((x * r) * gain + shift).astype(jnp.bfloat16)

    def project(i, k):
        return jnp.dot(h_ref[i % 2, 0:sizes[i], :], wi_ref[:, k * blk:(k + 1) * blk],
                       preferred_element_type=jnp.float32)

    def lagged(e, lag):
        return pltpu.roll(e, lag, axis=0)[HALO:, :]

    def mix_conv(i, d, res):
        cols = slice(d * gd, (d + 1) * gd)
        u, c, b_a, gate = (res[:, k * gd:(k + 1) * gd] for k in range(4))
        z = c * u
        ez = jnp.concatenate([zc_ref[:, cols], z], axis=0)
        zc_ref[:, cols] = z[sizes[i] - HALO:, :]
        conv = wc_ref[CONV_K - 1:CONV_K, cols] * z
        for k in range(CONV_K - 1):
            conv = conv + wc_ref[k:k + 1, cols] * lagged(ez, CONV_K - 1 - k)
        y_ref[i % 2, 0:sizes[i], cols] = (b_a * conv * _silu(gate)).astype(jnp.bfloat16)

    def mix_pool(i, pair, res):
        pooled = []
        for k in range(2):
            g = 2 * pair + k
            w = POOL_WINDOWS[g]
            cols = slice(g * gd, (g + 1) * gd)
            v = res[:, k * gd:(k + 1) * gd]
            acc = jnp.concatenate([pc_ref[:, cols], v], axis=0)
            pc_ref[:, cols] = v[sizes[i] - HALO:, :]
            span = 1
            while span < min(w, 8):
                acc = acc + pltpu.roll(acc, span, axis=0)
                span *= 2
            if span < w:
                acc = acc[8:, :] + acc[:-8, :]
                acc = acc[HALO - 8:, :]
            else:
                acc = acc[HALO:, :]
            if i == 0:
                pos = (lax.broadcasted_iota(jnp.int32, (HALO, gd), 0) + 1).astype(jnp.float32)
                inv = jnp.concatenate([1.0 / jnp.minimum(pos, float(w)),
                                       jnp.full((sizes[i] - HALO, gd), 1.0 / w, jnp.float32)], axis=0)
            else:
                inv = 1.0 / w
            pooled.append(acc * inv - v)
        mapped = jnp.dot(jnp.concatenate(pooled, axis=-1).astype(jnp.bfloat16), wp_ref[pair],
                         preferred_element_type=jnp.float32)
        return mapped, res[:, 2 * gd:4 * gd]

    def gate_pool(i, pair, mapped, gate):
        y_ref[i % 2, 0:sizes[i], cw + pair * 2 * gd:cw + (pair + 1) * 2 * gd] = (
            mapped * _silu(gate)).astype(jnp.bfloat16)

    def finish(i):
        rows = slice(starts[i], starts[i] + sizes[i])
        yo = jnp.dot(y_ref[i % 2, 0:sizes[i], :], wo_ref[...], preferred_element_type=jnp.float32)
        r2 = lax.rsqrt(jnp.mean(yo * yo, axis=-1, keepdims=True) + NORM_EPS)
        x_ref[rows, :] = x_ref[rows, :] + (yo * r2) * out_gain

    assert n_conv >= 2 and n_pair == 2
    seq = list(range(n_conv, n_conv + n_pair)) + list(range(n_conv))
    norm(0)
    head = [project(0, seq[0]), project(0, seq[1])]
    for i in range(n_chains):
        more = i + 1 < n_chains
        if more:
            norm(i + 1)
        res = head
        head = []
        mapped = []
        for pos, k in enumerate(seq):
            if pos + 2 < len(seq):
                res.append(project(i, seq[pos + 2]))
            elif more:
                head.append(project(i + 1, seq[pos + 2 - len(seq)]))
            if k >= n_conv:
                mapped.append(mix_pool(i, k - n_conv, res[pos]))
            else:
                mix_conv(i, k, res[pos])
                if k < n_pair:
                    gate_pool(i, k, *mapped[k])
        finish(i)

    o_copy(b, slot).start()

    @pl.when(last)
    def _():
        o_copy(b, prev_slot).wait()
        o_copy(b, slot).wait()


def _trunk(x, mod, g_pre, g_post, w_in, w_conv, w_pool, pool_scale, w_out):
    batch, seq, d = x.shape
    depth, _, in_cols = w_in.shape
    cw = w_conv.shape[2]
    _, groups, gd, _ = w_pool.shape
    pw = groups * gd
    chunk = WEIGHT_CHUNK_ROWS
    assert w_in.shape[1] == w_out.shape[1] == d and d % chunk == 0
    assert cw == pw and cw % gd == 0 and groups % 2 == 0
    assert d // chunk == batch and seq % SUB_ROWS == 0 and seq >= 2 * SUB_ROWS and batch * depth >= RING
    whole2 = lambda l, b: (0, 0)
    hbm = pl.BlockSpec(memory_space=pl.ANY)
    return pl.pallas_call(
        functools.partial(_trunk_kernel, cw=cw, gd=gd, sub=SUB_ROWS),
        grid=(depth, batch),
        in_specs=[
            hbm,
            pl.BlockSpec((None, batch, 3 * d), lambda l, b: (l, 0, 0)),
            pl.BlockSpec((depth, d), whole2),
            pl.BlockSpec((depth, d), whole2),
            hbm,
            pl.BlockSpec(w_conv.shape, lambda l, b: (0, 0, 0)),
            pl.BlockSpec(w_pool.shape, lambda l, b: (0, 0, 0, 0)),
            pl.BlockSpec((depth, pw), whole2),
            hbm,
        ],
        out_specs=hbm,
        out_shape=jax.ShapeDtypeStruct(x.shape, x.dtype),
        scratch_shapes=[
            pltpu.VMEM((RING, seq, d), jnp.float32),
            pltpu.VMEM((2, d, in_cols), jnp.bfloat16),
            pltpu.VMEM((2, d, w_out.shape[2]), jnp.bfloat16),
            pltpu.VMEM((2, groups // 2, 2 * gd, 2 * gd), jnp.bfloat16),
            pltpu.VMEM((2, chunk, in_cols), jnp.float32),
            pltpu.VMEM((2, chunk, w_out.shape[2]), jnp.float32),
            pltpu.VMEM((HALO, cw), jnp.float32),
            pltpu.VMEM((HALO, pw), jnp.float32),
            pltpu.VMEM((2, SUB_ROWS, d), jnp.bfloat16),
            pltpu.VMEM((2, SUB_ROWS, cw + pw), jnp.bfloat16),
            pltpu.SemaphoreType.DMA((RING,)),
            pltpu.SemaphoreType.DMA((RING,)),
            pltpu.SemaphoreType.DMA((2, 2)),
        ],
        compiler_params=pltpu.CompilerParams(
            dimension_semantics=("arbitrary", "arbitrary"),
            vmem_limit_bytes=VMEM_LIMIT_BYTES),
        name="mixer_trunk",
    )(x, mod, g_pre, g_post, w_in, w_conv, w_pool, pool_scale, w_out)


def kernel(x, c, w_ada, b_ada, g_pre, w_in, w_conv, w_pool, pool_scale, w_out, g_post):
    mod = _ada_modulation(c, w_ada, b_ada)
    return _trunk(x, mod, g_pre, g_post, w_in, w_conv, w_pool, pool_scale, w_out)
```
